```python
import math
import jax
import jax.numpy as jnp
from jax import lax
import numpy as np

D_MODEL = 1024
BATCH = 32
SEQ = 2048
DEPTH = 2
DEC_BATCH = 32
DEC_SEQ = 32
PAST_LEN = 1024

CHUNK = 64
N_EVEN = (DEPTH + 1) // 2
N_ODD = DEPTH // 2
FOX_HEAD_DIM = 64
FOX_HEADS = D_MODEL // 128
FOX_WIDTH = FOX_HEADS * FOX_HEAD_DIM
FORGET_BIAS = 3.0
POOL_WINDOWS = (2, 4, 8, 16)
POOL_GROUPS = len(POOL_WINDOWS)
POOL_WIDTH = D_MODEL // 2
POOL_GROUP_DIM = POOL_WIDTH // POOL_GROUPS
POOL_HIST = max(POOL_WINDOWS) - 1
EVEN_IN = 3 * FOX_WIDTH + FOX_HEADS + POOL_WIDTH
MIX_WIDTH = FOX_WIDTH + POOL_WIDTH
SGU_CHUNK = 128
SGU_WIDTH = D_MODEL
SGU_GROUPS = 8
SGU_GROUP_DIM = SGU_WIDTH // SGU_GROUPS
D_FF = ((8 * D_MODEL // 3) + 255) // 256 * 256
Q_BLOCK = 128
EPS = 1e-6

kernel_name = "fox_pool_sgu_macaron_stream_step"


def rms_norm(x, g):
    xf = x.astype(jnp.float32)
    y = xf * lax.rsqrt(jnp.mean(xf * xf, axis=-1, keepdims=True) + EPS)
    return (y * g.astype(jnp.float32)).astype(x.dtype)


def swiglu(h, w_in, w_down):
    gate, up = jnp.split(h @ w_in, 2, axis=-1)
    return (jax.nn.silu(gate) * up) @ w_down


def even_proj(h, w_in, b_f):
    B, T, _ = h.shape
    z = h @ w_in
    q = z[..., :FOX_WIDTH].reshape(B, T, FOX_HEADS, FOX_HEAD_DIM)
    k = z[..., FOX_WIDTH:2 * FOX_WIDTH].reshape(B, T, FOX_HEADS, FOX_HEAD_DIM)
    v = z[..., 2 * FOX_WIDTH:3 * FOX_WIDTH].reshape(B, T, FOX_HEADS, FOX_HEAD_DIM)
    f_logit = z[..., 3 * FOX_WIDTH:3 * FOX_WIDTH + FOX_HEADS] + b_f
    logf = jax.nn.log_sigmoid(f_logit.astype(jnp.float32))
    u = z[..., 3 * FOX_WIDTH + FOX_HEADS:]
    return q, k, v, logf, u


def fox_attend(q, cq, qpos, k, ck, kpos, v):
    s = jnp.einsum('bqhd,bkhd->bhqk', q, k).astype(jnp.float32) * (FOX_HEAD_DIM ** -0.5)
    s = s + jnp.transpose(cq, (0, 2, 1))[:, :, :, None] - jnp.transpose(ck, (0, 2, 1))[:, :, None, :]
    s = jnp.where(kpos[None, :] <= qpos[:, None], s, -jnp.inf)
    p = jax.nn.softmax(s, axis=-1)
    return jnp.einsum('bhqk,bkhd->bqhd', p.astype(v.dtype), v)


def fox_prompt(q, k, v, logf):
    B, S, H, Dh = q.shape
    c = jnp.cumsum(logf.astype(jnp.float32), axis=1)
    nb = S // Q_BLOCK
    pos = jnp.arange(S)
    qb = q.reshape(B, nb, Q_BLOCK, H, Dh).transpose(1, 0, 2, 3, 4)
    cb = c.reshape(B, nb, Q_BLOCK, H).transpose(1, 0, 2, 3)
    pb = pos.reshape(nb, Q_BLOCK)
    out = lax.map(lambda a: fox_attend(a[0], a[1], a[2], k, c, pos, v), (qb, cb, pb))
    return out.transpose(1, 0, 2, 3, 4).reshape(B, S, H * Dh)


def fox_sample(q, k, v, logf, k_past, v_past, logf_past):
    B, T, H, Dh = q.shape
    P = k_past.shape[1]
    k_all = jnp.concatenate([k_past, k], axis=1)
    v_all = jnp.concatenate([v_past, v], axis=1)
    c = jnp.cumsum(jnp.concatenate([logf_past.astype(jnp.float32), logf], axis=1), axis=1)
    kpos = jnp.arange(P + T)
    qpos = P + jnp.arange(T)
    out = fox_attend(q, c[:, P:], qpos, k_all, c, kpos, v_all)
    return out.reshape(B, T, H * Dh)


def pool_mix(u_ext, pos, w_grp, scale):
    B = u_ext.shape[0]
    T = pos.shape[0]
    uf = u_ext.astype(jnp.float32)
    cs = jnp.cumsum(jnp.pad(uf, ((0, 0), (1, 0), (0, 0))), axis=1)
    end = cs[:, POOL_HIST + 1:]
    u_new = uf[:, POOL_HIST:]
    outs = []
    for g, w in enumerate(POOL_WINDOWS):
        sl = slice(g * POOL_GROUP_DIM, (g + 1) * POOL_GROUP_DIM)
        win_sum = end[:, :, sl] - cs[:, POOL_HIST + 1 - w:POOL_HIST + 1 - w + T, sl]
        cnt = jnp.minimum(w, pos + 1).astype(jnp.float32)[None, :, None]
        outs.append(win_sum / cnt - u_new[:, :, sl])
    d = jnp.stack(outs, axis=2)
    y = jnp.einsum('btgc,gcd->btgd', d, w_grp.astype(jnp.float32)).reshape(B, T, POOL_WIDTH)
    return (y * scale.astype(jnp.float32)).astype(u_ext.dtype)


def sgu_mask(n):
    idx = jnp.arange(n) // CHUNK
    return idx[None, :] <= idx[:, None]


def sgu_proj(h, w_in, norm_g):
    zu, zv = jnp.split(jax.nn.gelu(h @ w_in), 2, axis=-1)
    return zu, rms_norm(zv, norm_g)


def sgu_prompt(zu, zv, w_s, b_s):
    B, S, _ = zu.shape
    n = S // SGU_CHUNK
    w = jnp.where(sgu_mask(SGU_CHUNK)[None], w_s, 0)
    vb = zv.reshape(B, n, SGU_CHUNK, SGU_GROUPS, SGU_GROUP_DIM)
    mix = jnp.einsum('gts,bnsgc->bntgc', w, vb) + b_s.T[:, :, None]
    return zu * mix.reshape(B, S, SGU_WIDTH)


def sgu_sample(zu, zv, w_s, b_s):
    B, T, _ = zu.shape
    w = jnp.where(sgu_mask(T)[None], w_s[:, :T, :T], 0)
    vb = zv.reshape(B, T, SGU_GROUPS, SGU_GROUP_DIM)
    mix = jnp.einsum('gts,bsgc->btgc', w, vb) + b_s[:, :T].T[:, :, None]
    return zu * mix.reshape(B, T, SGU_WIDTH)


def setup_inputs(seed: int = 0) -> dict:
    key = jax.random.key(seed)
    ks = jax.random.split(key, 20)
    f32 = jnp.float32

    def nrm(k, shape, scale):
        return jax.random.normal(k, shape, f32) * scale

    return {
        "x_prompt": nrm(ks[0], (BATCH, SEQ, D_MODEL), 1.0),
        "x_sample": nrm(ks[1], (DEC_BATCH, DEC_SEQ, D_MODEL), 1.0),
        "cache_k": nrm(ks[2], (N_EVEN, DEC_BATCH, PAST_LEN, FOX_HEADS, FOX_HEAD_DIM), 1.0),
        "cache_v": nrm(ks[3], (N_EVEN, DEC_BATCH, PAST_LEN, FOX_HEADS, FOX_HEAD_DIM), 1.0),
        "cache_logf": jax.nn.log_sigmoid(FORGET_BIAS + nrm(ks[4], (N_EVEN, DEC_BATCH, PAST_LEN, FOX_HEADS), 0.5)),
        "state_pool": nrm(ks[5], (N_EVEN, DEC_BATCH, POOL_HIST, POOL_WIDTH), 1.0),
        "norm_g": 1.0 + nrm(ks[6], (DEPTH, 3, D_MODEL), 0.05),
        "ffn_w_in": nrm(ks[7], (DEPTH, 2, D_MODEL, 2 * D_FF), D_MODEL ** -0.5),
        "ffn_w_down": nrm(ks[8], (DEPTH, 2, D_FF, D_MODEL), D_FF ** -0.5),
        "even_w_in": nrm(ks[9], (N_EVEN, D_MODEL, EVEN_IN), D_MODEL ** -0.5),
        "even_b_f": FORGET_BIAS + nrm(ks[10], (N_EVEN, FOX_HEADS), 0.5),
        "pool_w": nrm(ks[11], (N_EVEN, POOL_GROUPS, POOL_GROUP_DIM, POOL_GROUP_DIM), POOL_GROUP_DIM ** -0.5),
        "pool_scale": 1.0 + nrm(ks[12], (N_EVEN, POOL_WIDTH), 0.1),
        "even_w_out": nrm(ks[13], (N_EVEN, MIX_WIDTH, D_MODEL), MIX_WIDTH ** -0.5),
        "sgu_w_in": nrm(ks[14], (N_ODD, D_MODEL, 2 * SGU_WIDTH), D_MODEL ** -0.5),
        "sgu_norm_g": 1.0 + nrm(ks[15], (N_ODD, SGU_WIDTH), 0.05),
        "sgu_w_s": nrm(ks[16], (N_ODD, SGU_GROUPS, SGU_CHUNK, SGU_CHUNK), SGU_CHUNK ** -0.5),
        "sgu_b_s": 1.0 + nrm(ks[17], (N_ODD, SGU_GROUPS, SGU_CHUNK), 0.1),
        "sgu_w_out": nrm(ks[18], (N_ODD, SGU_WIDTH, D_MODEL), SGU_WIDTH ** -0.5),
        "final_g": 1.0 + nrm(ks[19], (D_MODEL,), 0.05),
    }


def reference(x_prompt, x_sample, cache_k, cache_v, cache_logf, state_pool, norm_g, ffn_w_in, ffn_w_down,
              even_w_in, even_b_f, pool_w, pool_scale, even_w_out, sgu_w_in, sgu_norm_g, sgu_w_s, sgu_b_s,
              sgu_w_out, final_g):
    xp, xs = x_prompt, x_sample
    B, S, _ = xp.shape
    Bs, T, _ = xs.shape
    P = cache_k.shape[2]
    pos_p = jnp.arange(S)
    pos_s = P + jnp.arange(T)
    kp_l, vp_l, fp_l, up_l = [], [], [], []
    ks_l, vs_l, fs_l, us_l, zs_l = [], [], [], [], []
    for l in range(DEPTH):
        xp = xp + 0.5 * swiglu(rms_norm(xp, norm_g[l, 0]), ffn_w_in[l, 0], ffn_w_down[l, 0])
        xs = xs + 0.5 * swiglu(rms_norm(xs, norm_g[l, 0]), ffn_w_in[l, 0], ffn_w_down[l, 0])
        hp = rms_norm(xp, norm_g[l, 1])
        hs = rms_norm(xs, norm_g[l, 1])
        if l % 2 == 0:
            e = l // 2
            q, k, v, lf, u = even_proj(hp, even_w_in[e], even_b_f[e])
            att = fox_prompt(q, k, v, lf)
            u_ext = jnp.concatenate([jnp.zeros((B, POOL_HIST, POOL_WIDTH), u.dtype), u], axis=1)
            pool = pool_mix(u_ext, pos_p, pool_w[e], pool_scale[e])
            xp = xp + jnp.concatenate([att, pool], axis=-1) @ even_w_out[e]
            kp_l.append(k); vp_l.append(v); fp_l.append(lf); up_l.append(u_ext[:, -POOL_HIST:])
            q, k, v, lf, u = even_proj(hs, even_w_in[e], even_b_f[e])
            att = fox_sample(q, k, v, lf, cache_k[e], cache_v[e], cache_logf[e])
            u_ext = jnp.concatenate([state_pool[e], u], axis=1)
            pool = pool_mix(u_ext, pos_s, pool_w[e], pool_scale[e])
            xs = xs + jnp.concatenate([att, pool], axis=-1) @ even_w_out[e]
            ks_l.append(k); vs_l.append(v); fs_l.append(lf); us_l.append(u_ext[:, -POOL_HIST:])
        else:
            o = l // 2
            zu, zv = sgu_proj(hp, sgu_w_in[o], sgu_norm_g[o])
            xp = xp + sgu_prompt(zu, zv, sgu_w_s[o], sgu_b_s[o]) @ sgu_w_out[o]
            zu, zv = sgu_proj(hs, sgu_w_in[o], sgu_norm_g[o])
            xs = xs + sgu_sample(zu, zv, sgu_w_s[o], sgu_b_s[o]) @ sgu_w_out[o]
            zs_l.append(zv)
        xp = xp + 0.5 * swiglu(rms_norm(xp, norm_g[l, 2]), ffn_w_in[l, 1], ffn_w_down[l, 1])
        xs = xs + 0.5 * swiglu(rms_norm(xs, norm_g[l, 2]), ffn_w_in[l, 1], ffn_w_down[l, 1])
    y_prompt = rms_norm(xp, final_g)
    y_sample = rms_norm(xs, final_g)
    return (y_prompt, y_sample, jnp.stack(kp_l), jnp.stack(vp_l), jnp.stack(fp_l), jnp.stack(up_l),
            jnp.stack(ks_l), jnp.stack(vs_l), jnp.stack(fs_l), jnp.stack(us_l), jnp.stack(zs_l))
```

```python
import functools
import math

import jax
import jax.numpy as jnp
from jax import lax
from jax.experimental import pallas as pl
from jax.experimental.pallas import tpu as pltpu

F32 = jnp.float32
BF16 = jnp.bfloat16

EPS = 1e-6
FOX_HEAD_DIM = 64
POOL_WINDOWS = (2, 4, 8, 16)
POOL_HIST = max(POOL_WINDOWS) - 1
SGU_STREAM_CHUNK = 64
SGU_CHUNK = 128

V7X_LANES = 128
V7X_MXU_DIM = 256
V7X_VMEM_BYTES = 64 * 1024 * 1024
VMEM_COMPILER_RESERVE = 8 * 1024 * 1024

ROW_TILE = 512
FF_CHUNK = V7X_MXU_DIM
ATT_Q_BLOCK = 256
AUG_LANES = V7X_LANES


def _vmem_limit(est_bytes):
    return int(min(max(est_bytes, 16 * 1024 * 1024), V7X_VMEM_BYTES - VMEM_COMPILER_RESERVE))


def _resident(shape):
    nd = len(shape)
    return pl.BlockSpec(shape, lambda *_: (0,) * nd, pipeline_mode=pl.Buffered(1))


def _rms(x, g):
    ms = jnp.mean(x * x, axis=-1, keepdims=True)
    return x * lax.rsqrt(ms + EPS) * g


def _round_up(n, m):
    return (n + m - 1) // m * m


def _ffn_body(n_mix, n_chunks, has_final, *refs):
    x_ref = refs[0]
    a_refs = refs[1:1 + n_mix]
    w_refs = refs[1 + n_mix:1 + 2 * n_mix]
    g_ref, win_ref, wdown_ref = refs[1 + 2 * n_mix:4 + 2 * n_mix]
    pos = 4 + 2 * n_mix
    fg_ref = refs[pos] if has_final else None
    pos += int(has_final)
    o_ref, h_scr, act_scr = refs[pos:pos + 3]

    x = x_ref[...]
    for a_ref, w_ref in zip(a_refs, w_refs):
        x = x + jnp.dot(a_ref[...], w_ref[...], preferred_element_type=F32)
    o_ref[...] = x
    h_scr[...] = _rms(x, g_ref[...]).astype(BF16)
    for j in range(n_chunks):
        gu = jnp.dot(h_scr[...], win_ref[j], preferred_element_type=F32)
        gate = gu[:, :FF_CHUNK]
        up = gu[:, FF_CHUNK:]
        act = gate * (1.0 / (1.0 + jnp.exp(-gate))) * up
        act_scr[:, j * FF_CHUNK:(j + 1) * FF_CHUNK] = act.astype(BF16)
    y = jnp.dot(act_scr[...], wdown_ref[...], preferred_element_type=F32)
    out = o_ref[...] + 0.5 * y
    if has_final:
        out = _rms(out, fg_ref[...])
    o_ref[...] = out


def _ffn(x, mixes, g, win3, wdown, final_g=None):
    rows, d = x.shape
    n_chunks, _, two_chunk = win3.shape
    d_ff = wdown.shape[0]
    tm = ROW_TILE
    n_mix = len(mixes)
    has_final = final_g is not None

    row_spec = lambda width: pl.BlockSpec((tm, width), lambda i: (i, 0))
    in_specs = [row_spec(d)]
    in_specs += [row_spec(a.shape[1]) for a, _ in mixes]
    in_specs += [_resident(w.shape) for _, w in mixes]
    in_specs += [_resident((1, d)), _resident(win3.shape), _resident(wdown.shape)]
    args = [x] + [a for a, _ in mixes] + [w for _, w in mixes] + [g.reshape(1, d), win3, wdown]
    if has_final:
        in_specs.append(_resident((1, d)))
        args.append(final_g.reshape(1, d))

    est = (4 * tm * d * 4
           + sum(2 * tm * a.shape[1] * 2 + w.size * 2 for a, w in mixes)
           + (win3.size + wdown.size) * 2
           + tm * d * 2 + tm * d_ff * 2
           + tm * two_chunk * 4 * 2 + tm * d * 4 * 2)
    return pl.pallas_call(
        functools.partial(_ffn_body, n_mix, n_chunks, has_final),
        grid=(rows // tm,),
        in_specs=in_specs,
        out_specs=row_spec(d),
        out_shape=jax.ShapeDtypeStruct((rows, d), F32),
        scratch_shapes=[pltpu.VMEM((tm, d), BF16), pltpu.VMEM((tm, d_ff), BF16)],
        compiler_params=pltpu.CompilerParams(
            dimension_semantics=("arbitrary",), vmem_limit_bytes=_vmem_limit(est)),
        name="ffn_half_step",
    )(*args)


def _even_proj_body(width, heads, x_ref, g_ref, w_ref, bf_ref, q_ref, k_ref, v_ref, u_ref, lf_ref):
    h = _rms(x_ref[...], g_ref[...]).astype(BF16)
    z = jnp.dot(h, w_ref[...], preferred_element_type=F32)
    q_ref[...] = (z[:, :width] * (FOX_HEAD_DIM ** -0.5)).astype(BF16)
    k_ref[...] = z[:, width:2 * width]
    v_ref[...] = z[:, 2 * width:3 * width]
    u_ref[...] = z[:, 3 * width:4 * width]
    f = z[:, 4 * width:4 * width + V7X_LANES] + bf_ref[...]
    logf = jnp.minimum(f, 0.0) - jnp.log(1.0 + jnp.exp(-jnp.abs(f)))
    lf_ref[...] = logf[:, :heads]


def _even_proj(x, g, w_all, bf_pad, width, heads):
    rows, d = x.shape
    tm = ROW_TILE
    n_out = w_all.shape[1]
    row_spec = lambda wd: pl.BlockSpec((tm, wd), lambda i: (i, 0))
    est = (2 * tm * d * 4 + w_all.size * 2 + 2 * tm * width * (2 + 3 * 4)
           + 2 * tm * V7X_LANES * 4 + tm * n_out * 4 * 2 + tm * d * 2)
    return pl.pallas_call(
        functools.partial(_even_proj_body, width, heads),
        grid=(rows // tm,),
        in_specs=[row_spec(d), _resident((1, d)), _resident(w_all.shape), _resident((1, V7X_LANES))],
        out_specs=[row_spec(width), row_spec(width), row_spec(width), row_spec(width), row_spec(heads)],
        out_shape=[jax.ShapeDtypeStruct((rows, width), BF16),
                   jax.ShapeDtypeStruct((rows, width), F32),
                   jax.ShapeDtypeStruct((rows, width), F32),
                   jax.ShapeDtypeStruct((rows, width), F32),
                   jax.ShapeDtypeStruct((rows, heads), F32)],
        compiler_params=pltpu.CompilerParams(
            dimension_semantics=("arbitrary",), vmem_limit_bytes=_vmem_limit(est)),
        name="even_projection",
    )(x, g.reshape(1, d), w_all, bf_pad)


def _split3(c):
    hi = c.astype(BF16).astype(F32)
    r1 = c - hi
    mid = r1.astype(BF16).astype(F32)
    lo = (r1 - mid).astype(BF16).astype(F32)
    return hi, mid, lo


def _attn_body(past, t_new, tq, heads, *refs):
    n_all = past + t_new
    n_pad = _round_up(n_all, V7X_LANES)
    n_pairs = heads // 2
    if past:
        (q_ref, k_ref, v_ref, lf_ref, kp_ref, vp_ref, lfp_ref, u_ref, hist_ref, pw_ref, ps_ref,
         att_ref, pool_ref, ca, cb, c_scr, c2_scr, kcat, qcat, vb, pa, pb) = refs
    else:
        (q_ref, k_ref, v_ref, lf_ref, u_ref, hist_ref, pw_ref, ps_ref,
         att_ref, pool_ref, ca, cb, c_scr, c2_scr, kcat, qcat, vb, pa, pb) = refs
    pair = pl.program_id(1)

    @pl.when(pair == 0)
    def _():
        pad = 8
        ca[0:pad, :] = jnp.zeros((pad, heads), F32)
        cb[0:pad, :] = jnp.zeros((pad, heads), F32)
        if past:
            ca[pad:pad + past, :] = lfp_ref[...]
        ca[pad + past:pad + n_all, :] = lf_ref[...]
        src, dst = ca, cb
        s = 1
        while s < n_all:
            if s < pad:
                dst[pad:pad + n_all, :] = src[pad:pad + n_all, :] + src[pad - s:pad + n_all - s, :]
            else:
                dst[pad:pad + s, :] = src[pad:pad + s, :]
                dst[pad + s:pad + n_all, :] = src[pad + s:pad + n_all, :] + src[pad:pad + n_all - s, :]
            src, dst = dst, src
            s *= 2
        c_scr[...] = src[pad:pad + n_all, :]

    for pp in range(n_pairs):
        @pl.when(pair == pp)
        def _(pp=pp):
            c2_scr[:, 0:V7X_LANES] = jnp.broadcast_to(c_scr[:, 2 * pp:2 * pp + 1], (n_all, V7X_LANES))
            c2_scr[:, V7X_LANES:2 * V7X_LANES] = jnp.broadcast_to(
                c_scr[:, 2 * pp + 1:2 * pp + 2], (n_all, V7X_LANES))

    lane_k = lax.broadcasted_iota(jnp.int32, (n_all, AUG_LANES), 1)
    hi0, mid0, lo0 = _split3(c2_scr[:, 0:V7X_LANES])
    hi1, mid1, lo1 = _split3(c2_scr[:, V7X_LANES:2 * V7X_LANES])
    aug_k = jnp.where(lane_k < 12, 1.0, 0.0)
    for ln, piece in ((3, hi0), (4, mid0), (5, lo0), (9, hi1), (10, mid1), (11, lo1)):
        aug_k = jnp.where(lane_k == ln, -piece, aug_k)
    aug_k = aug_k.astype(BF16)
    if past:
        kcat[0:past, 0:V7X_LANES] = kp_ref[...].astype(BF16)
        vb[0:past, :] = vp_ref[...].astype(BF16)
    kcat[past:n_all, 0:V7X_LANES] = k_ref[...].astype(BF16)
    vb[past:n_all, :] = v_ref[...].astype(BF16)
    kcat[0:n_all, V7X_LANES:V7X_LANES + AUG_LANES] = aug_k
    if n_pad > n_all:
        kcat[n_all:n_pad, :] = jnp.zeros((n_pad - n_all, V7X_LANES + AUG_LANES), BF16)
        vb[n_all:n_pad, :] = jnp.zeros((n_pad - n_all, V7X_LANES), BF16)

    lane_q = lax.broadcasted_iota(jnp.int32, (t_new, AUG_LANES), 1)
    qv = q_ref[...].astype(F32)
    for hh, (hi, mid, lo) in enumerate(((hi0, mid0, lo0), (hi1, mid1, lo1))):
        base = 6 * hh
        aug_q = jnp.where((lane_q >= base + 3) & (lane_q < base + 6), 1.0, 0.0)
        for off, piece in enumerate((hi, mid, lo)):
            aug_q = jnp.where(lane_q == base + off, piece[past:n_all], aug_q)
        in_head = (lane_q >= hh * FOX_HEAD_DIM) & (lane_q < (hh + 1) * FOX_HEAD_DIM)
        qcat[hh, :, 0:V7X_LANES] = jnp.where(in_head, qv, 0.0).astype(BF16)
        qcat[hh, :, V7X_LANES:V7X_LANES + AUG_LANES] = aug_q.astype(BF16)

    nt = (((1,), (1,)), ((), ()))
    lane_o = lax.broadcasted_iota(jnp.int32, (tq, V7X_LANES), 1)
    for i in range(t_new // tq):
        r0 = i * tq
        kend = _round_up(past + r0 + tq, V7X_LANES)
        d0 = (past + r0) // V7X_LANES * V7X_LANES
        row = lax.broadcasted_iota(jnp.int32, (tq, kend - d0), 0) + (past + r0)
        col = lax.broadcasted_iota(jnp.int32, (tq, kend - d0), 1) + d0
        visible = col <= row
        outs = []
        for hh in range(2):
            s = lax.dot_general(qcat[hh, r0:r0 + tq, :], kcat[0:kend, :], nt, preferred_element_type=F32)
            s_diag = jnp.where(visible, s[:, d0:kend], -jnp.inf)
            if d0 > 0:
                s_old = s[:, 0:d0]
                m = jnp.maximum(jnp.max(s_old, axis=1, keepdims=True), jnp.max(s_diag, axis=1, keepdims=True))
                p_old = jnp.exp(s_old - m)
                p_diag = jnp.exp(s_diag - m)
                denom = jnp.sum(p_old, axis=1, keepdims=True) + jnp.sum(p_diag, axis=1, keepdims=True)
                o = (jnp.dot(p_old.astype(BF16), vb[0:d0, :], preferred_element_type=F32)
                     + jnp.dot(p_diag.astype(BF16), vb[d0:kend, :], preferred_element_type=F32))
            else:
                m = jnp.max(s_diag, axis=1, keepdims=True)
                p_diag = jnp.exp(s_diag - m)
                denom = jnp.sum(p_diag, axis=1, keepdims=True)
                o = jnp.dot(p_diag.astype(BF16), vb[d0:kend, :], preferred_element_type=F32)
            outs.append(o * (1.0 / denom))
        att_ref[r0:r0 + tq, :] = jnp.where(lane_o < FOX_HEAD_DIM, outs[0], outs[1]).astype(BF16)

    lead = 2 * (POOL_HIST + 1)
    half = POOL_HIST + 1
    pa[0:half, :] = jnp.zeros((half, V7X_LANES), F32)
    pb[0:half, :] = jnp.zeros((half, V7X_LANES), F32)
    pa[half:lead, :] = hist_ref[...]
    pa[lead:lead + t_new, :] = u_ref[...]
    src, dst = pa, pb
    win = None
    shift = 1
    for gi in range(len(POOL_WINDOWS)):
        dst[half:lead + t_new, :] = src[half:lead + t_new, :] + src[half - shift:lead + t_new - shift, :]
        cur = dst[lead:lead + t_new, :]
        win = cur if win is None else jnp.where(pair >= gi, cur, win)
        src, dst = dst, src
        shift *= 2
    width = jnp.left_shift(2, pair)
    posn = lax.broadcasted_iota(jnp.int32, (t_new, V7X_LANES), 0) + past
    cnt = jnp.minimum(width, posn + 1).astype(F32)
    dlt = win / cnt - u_ref[...]
    y = jnp.dot(dlt.astype(BF16), pw_ref[...], preferred_element_type=F32)
    pool_ref[...] = (y * ps_ref[...]).astype(BF16)


def _attention_pool(q, k, v, logf, u, hist16, pool_w, pool_scale, past_kvf=None):
    b, t_new, width = q.shape
    heads = logf.shape[-1]
    n_pairs = width // V7X_LANES
    past = 0 if past_kvf is None else past_kvf[0].shape[1]
    n_all = past + t_new
    n_pad = _round_up(n_all, V7X_LANES)
    tq = min(ATT_Q_BLOCK, t_new)

    slab = lambda rows: pl.BlockSpec((None, rows, V7X_LANES), lambda bi, pi: (bi, 0, pi))
    gates = lambda rows: pl.BlockSpec((None, rows, heads), lambda bi, pi: (bi, 0, 0))
    in_specs = [slab(t_new), slab(t_new), slab(t_new), gates(t_new)]
    args = [q, k, v, logf]
    if past:
        in_specs += [slab(past), slab(past), gates(past)]
        args += list(past_kvf)
    in_specs += [slab(t_new), slab(POOL_HIST + 1),
                 pl.BlockSpec((None, V7X_LANES, V7X_LANES), lambda bi, pi: (pi, 0, 0)),
                 pl.BlockSpec((None, 1, V7X_LANES), lambda bi, pi: (pi, 0, 0))]
    args += [u, hist16, pool_w, pool_scale]

    lane_pad_f32 = V7X_LANES * 4
    scratch = [
        pltpu.VMEM((8 + n_all, heads), F32), pltpu.VMEM((8 + n_all, heads), F32),
        pltpu.VMEM((n_all, heads), F32), pltpu.VMEM((n_all, 2 * V7X_LANES), F32),
        pltpu.VMEM((n_pad, V7X_LANES + AUG_LANES), BF16),
        pltpu.VMEM((2, t_new, V7X_LANES + AUG_LANES), BF16),
        pltpu.VMEM((n_pad, V7X_LANES), BF16),
        pltpu.VMEM((2 * (POOL_HIST + 1) + t_new, V7X_LANES), F32),
        pltpu.VMEM((2 * (POOL_HIST + 1) + t_new, V7X_LANES), F32),
    ]
    est = (2 * t_new * V7X_LANES * (2 + 4 + 4 + 4 + 2 + 2) + 2 * t_new * lane_pad_f32
           + 2 * past * (2 * V7X_LANES * 4 + lane_pad_f32)
           + 3 * (8 + n_all) * lane_pad_f32 + n_all * 2 * lane_pad_f32
           + n_pad * (V7X_LANES + AUG_LANES) * 2 + 2 * t_new * (V7X_LANES + AUG_LANES) * 2
           + n_pad * V7X_LANES * 2 + 2 * (32 + t_new) * lane_pad_f32
           + 6 * tq * n_pad * 4 + 8 * n_all * lane_pad_f32)
    return pl.pallas_call(
        functools.partial(_attn_body, past, t_new, tq, heads),
        grid=(b, n_pairs),
        in_specs=in_specs,
        out_specs=[slab(t_new), slab(t_new)],
        out_shape=[jax.ShapeDtypeStruct((b, t_new, width), BF16),
                   jax.ShapeDtypeStruct((b, t_new, width), BF16)],
        scratch_shapes=scratch,
        compiler_params=pltpu.CompilerParams(
            dimension_semantics=("arbitrary", "arbitrary"), vmem_limit_bytes=_vmem_limit(est)),
        name="fox_attention_pool",
    )(*args)


def _sgu_body(groups, want_zv, x_ref, g_ref, w_ref, ng_ref, ws_ref, bias_ref, *rest):
    if want_zv:
        o_ref, zv_ref, zvb_scr = rest
    else:
        o_ref, zvb_scr = rest
        zv_ref = None
    tm = x_ref.shape[0]
    width = ng_ref.shape[1]
    gdim = width // groups
    h = _rms(x_ref[...], g_ref[...]).astype(BF16)
    z = jnp.dot(h, w_ref[...], preferred_element_type=F32)
    z = 0.5 * z * (1.0 + jnp.tanh(math.sqrt(2.0 / math.pi) * (z + 0.044715 * (z * z * z))))
    zv = _rms(z[:, width:], ng_ref[...])
    if want_zv:
        zv_ref[...] = zv
    zvb_scr[...] = zv.astype(BF16)
    for ch in range(tm // SGU_CHUNK):
        rs = slice(ch * SGU_CHUNK, (ch + 1) * SGU_CHUNK)
        for gi in range(groups):
            cs = slice(gi * gdim, (gi + 1) * gdim)
            mix = jnp.dot(ws_ref[gi], zvb_scr[rs, cs], preferred_element_type=F32) + bias_ref[:, cs]
            o_ref[rs, cs] = (z[rs, cs] * mix).astype(BF16)


def _sgu(x, g, w_in, norm_g, ws_mat, bias_full, want_zv):
    rows, d = x.shape
    width = norm_g.shape[0]
    groups = ws_mat.shape[0]
    tm = ROW_TILE
    row_spec = lambda wd: pl.BlockSpec((tm, wd), lambda i: (i, 0))
    out_specs = [row_spec(width)]
    out_shape = [jax.ShapeDtypeStruct((rows, width), BF16)]
    if want_zv:
        out_specs.append(row_spec(width))
        out_shape.append(jax.ShapeDtypeStruct((rows, width), F32))
    est = (2 * tm * d * 4 + w_in.size * 2 + ws_mat.size * 2 + bias_full.size * 4
           + 2 * tm * width * 2 + (2 * tm * width * 4 if want_zv else 0)
           + tm * width * 2 + 3 * tm * 2 * width * 4)
    res = pl.pallas_call(
        functools.partial(_sgu_body, groups, want_zv),
        grid=(rows // tm,),
        in_specs=[row_spec(d), _resident((1, d)), _resident(w_in.shape), _resident((1, width)),
                  _resident(ws_mat.shape), _resident(bias_full.shape)],
        out_specs=out_specs,
        out_shape=out_shape,
        scratch_shapes=[pltpu.VMEM((tm, width), BF16)],
        compiler_params=pltpu.CompilerParams(
            dimension_semantics=("arbitrary",), vmem_limit_bytes=_vmem_limit(est)),
        name="spatial_gating",
    )(x, g.reshape(1, d), w_in, norm_g.reshape(1, width), ws_mat, bias_full)
    return res if want_zv else (res[0], None)


def _prep_ffn(w_in, w_down):
    d, two_ff = w_in.shape
    d_ff = two_ff // 2
    n_chunks = d_ff // FF_CHUNK
    gate = w_in[:, :d_ff].reshape(d, n_chunks, FF_CHUNK)
    up = w_in[:, d_ff:].reshape(d, n_chunks, FF_CHUNK)
    win3 = jnp.concatenate([gate, up], axis=-1).transpose(1, 0, 2).astype(BF16)
    return win3, w_down.astype(BF16)


def _sgu_spatial(w_s, b_s, seq_len):
    groups = w_s.shape[0]
    span = min(seq_len, SGU_CHUNK)
    idx = jnp.arange(span) // SGU_STREAM_CHUNK
    w = jnp.where((idx[None, :] <= idx[:, None])[None], w_s[:, :span, :span], 0.0)
    reps = SGU_CHUNK // span
    if reps > 1:
        w = jnp.einsum('ab,gts->gatbs', jnp.eye(reps, dtype=w.dtype), w).reshape(groups, SGU_CHUNK, SGU_CHUNK)
    bias = jnp.tile(b_s[:, :span], (1, reps))
    return w.astype(BF16), bias


def kernel(x_prompt, x_sample, cache_k, cache_v, cache_logf, state_pool, norm_g, ffn_w_in, ffn_w_down,
           even_w_in, even_b_f, pool_w, pool_scale, even_w_out, sgu_w_in, sgu_norm_g, sgu_w_s, sgu_b_s,
           sgu_w_out, final_g):
    b, s, d = x_prompt.shape
    bs, t, _ = x_sample.shape
    depth = norm_g.shape[0]
    heads = even_b_f.shape[1]
    width = heads * FOX_HEAD_DIM
    pool_width = pool_scale.shape[1]
    n_groups = pool_w.shape[1]
    past = cache_k.shape[2]
    sgu_width = sgu_norm_g.shape[1]
    sgu_groups = sgu_w_s.shape[1]
    sgu_gdim = sgu_width // sgu_groups

    xp = x_prompt.reshape(b * s, d)
    xs = x_sample.reshape(bs * t, d)
    kp_l, vp_l, fp_l, up_l = [], [], [], []
    ks_l, vs_l, fs_l, us_l, zs_l = [], [], [], [], []

    for l in range(depth):
        win_a, wdown_a = _prep_ffn(ffn_w_in[l, 0], ffn_w_down[l, 0])
        win_b, wdown_b = _prep_ffn(ffn_w_in[l, 1], ffn_w_down[l, 1])
        last = final_g if l == depth - 1 else None
        xp = _ffn(xp, [], norm_g[l, 0], win_a, wdown_a)
        xs = _ffn(xs, [], norm_g[l, 0], win_a, wdown_a)
        if l % 2 == 0:
            e = l // 2
            w_e = even_w_in[e]
            f_cols = jnp.pad(w_e[:, 3 * width:3 * width + heads], ((0, 0), (0, V7X_LANES - heads)))
            w_all = jnp.concatenate([w_e[:, :3 * width], w_e[:, 3 * width + heads:], f_cols], axis=1).astype(BF16)
            bf_pad = jnp.pad(even_b_f[e], (0, V7X_LANES - heads)).reshape(1, V7X_LANES)
            pw = pool_w[e].astype(BF16)
            psc = pool_scale[e].reshape(n_groups, 1, pool_width // n_groups)
            w_out_att = even_w_out[e, :width].astype(BF16)
            w_out_pool = even_w_out[e, width:].astype(BF16)

            q, k, v, u, lf = _even_proj(xp, norm_g[l, 1], w_all, bf_pad, width, heads)
            hist = jnp.zeros((b, POOL_HIST + 1, pool_width), F32)
            att, pool = _attention_pool(q.reshape(b, s, width), k.reshape(b, s, width), v.reshape(b, s, width),
                                        lf.reshape(b, s, heads), u.reshape(b, s, pool_width), hist, pw, psc)
            xp = _ffn(xp, [(att.reshape(b * s, width), w_out_att), (pool.reshape(b * s, pool_width), w_out_pool)],
                      norm_g[l, 2], win_b, wdown_b, last)
            kp_l.append(k.reshape(b, s, heads, FOX_HEAD_DIM))
            vp_l.append(v.reshape(b, s, heads, FOX_HEAD_DIM))
            fp_l.append(lf.reshape(b, s, heads))
            up_l.append(u.reshape(b, s, pool_width)[:, s - POOL_HIST:])

            q, k, v, u, lf = _even_proj(xs, norm_g[l, 1], w_all, bf_pad, width, heads)
            hist = jnp.pad(state_pool[e], ((0, 0), (1, 0), (0, 0)))
            past_kvf = (cache_k[e].reshape(bs, past, width), cache_v[e].reshape(bs, past, width), cache_logf[e])
            att, pool = _attention_pool(q.reshape(bs, t, width), k.reshape(bs, t, width), v.reshape(bs, t, width),
                                        lf.reshape(bs, t, heads), u.reshape(bs, t, pool_width), hist, pw, psc,
                                        past_kvf)
            xs = _ffn(xs, [(att.reshape(bs * t, width), w_out_att), (pool.reshape(bs * t, pool_width), w_out_pool)],
                      norm_g[l, 2], win_b, wdown_b, last)
            ks_l.append(k.reshape(bs, t, heads, FOX_HEAD_DIM))
            vs_l.append(v.reshape(bs, t, heads, FOX_HEAD_DIM))
            fs_l.append(lf.reshape(bs, t, heads))
            u_ext = jnp.concatenate([state_pool[e], u.reshape(bs, t, pool_width)], axis=1)
            us_l.append(u_ext[:, -POOL_HIST:])
        else:
            o = l // 2
            w_in = sgu_w_in[o].astype(BF16)
            w_out = sgu_w_out[o].astype(BF16)
            ws_p, bias_p = _sgu_spatial(sgu_w_s[o], sgu_b_s[o], s)
            ws_s, bias_s = _sgu_spatial(sgu_w_s[o], sgu_b_s[o], t)
            expand = lambda bias: jnp.repeat(bias.T, sgu_gdim, axis=1)
            gated, _ = _sgu(xp, norm_g[l, 1], w_in, sgu_norm_g[o], ws_p, expand(bias_p), False)
            xp = _ffn(xp, [(gated, w_out)], norm_g[l, 2], win_b, wdown_b, last)
            gated, zv = _sgu(xs, norm_g[l, 1], w_in, sgu_norm_g[o], ws_s, expand(bias_s), True)
            xs = _ffn(xs, [(gated, w_out)], norm_g[l, 2], win_b, wdown_b, last)
            zs_l.append(zv.reshape(bs, t, sgu_width))

    return (xp.reshape(b, s, d), xs.reshape(bs, t, d),
            jnp.stack(kp_l), jnp.stack(vp_l), jnp.stack(fp_l), jnp.stack(up_l),
            jnp.stack(ks_l), jnp.stack(vs_l), jnp.stack(fs_l), jnp.stack(us_l), jnp.stack(zs_l))
```

```python
import functools
import math

import jax
import jax.numpy as jnp
from jax import lax
from jax.experimental import pallas as pl
from jax.experimental.pallas import tpu as pltpu

F32 = jnp.float32
BF16 = jnp.bfloat16

EPS = 1e-6
FOX_HEAD_DIM = 64
POOL_WINDOWS = (2, 4, 8, 16)
POOL_HIST = max(POOL_WINDOWS) - 1
SGU_STREAM_CHUNK = 64
SGU_CHUNK = 128

V7X_LANES = 128
V7X_MXU_DIM = 256
V7X_VMEM_BYTES = 64 * 1024 * 1024
VMEM_COMPILER_RESERVE = 8 * 1024 * 1024

ROW_TILE = 512
FF_CHUNK = V7X_MXU_DIM
ATT_Q_BLOCK = 128
ATT_KV_CHUNK = V7X_MXU_DIM
AUG_LANES = V7X_LANES
N_SPLIT = 3
GATE_COPIES = 2 * N_SPLIT
LOG2E = math.log2(math.e)


def _vmem_limit(est_bytes):
    return int(min(max(est_bytes, 16 * 1024 * 1024), V7X_VMEM_BYTES - VMEM_COMPILER_RESERVE))


def _resident(shape):
    nd = len(shape)
    return pl.BlockSpec(shape, lambda *_: (0,) * nd, pipeline_mode=pl.Buffered(1))


def _rms(x, g):
    ms = jnp.mean(x * x, axis=-1, keepdims=True)
    return x * lax.rsqrt(ms + EPS) * g


def _round_up(n, m):
    return (n + m - 1) // m * m


def _ffn_body(n_mix, n_chunks, has_final, *refs):
    x_ref = refs[0]
    a_refs = refs[1:1 + n_mix]
    w_refs = refs[1 + n_mix:1 + 2 * n_mix]
    g_ref, win_ref, wdown_ref = refs[1 + 2 * n_mix:4 + 2 * n_mix]
    pos = 4 + 2 * n_mix
    fg_ref = refs[pos] if has_final else None
    pos += int(has_final)
    o_ref, h_scr, act_scr = refs[pos:pos + 3]

    x = x_ref[...]
    for a_ref, w_ref in zip(a_refs, w_refs):
        x = x + jnp.dot(a_ref[...], w_ref[...], preferred_element_type=F32)
    o_ref[...] = x
    h_scr[...] = _rms(x, g_ref[...]).astype(BF16)
    for j in range(n_chunks):
        gu = jnp.dot(h_scr[...], win_ref[j], preferred_element_type=F32)
        gate = gu[:, :FF_CHUNK]
        up = gu[:, FF_CHUNK:]
        act = gate * (1.0 / (1.0 + jnp.exp(-gate))) * up
        act_scr[:, j * FF_CHUNK:(j + 1) * FF_CHUNK] = act.astype(BF16)
    y = jnp.dot(act_scr[...], wdown_ref[...], preferred_element_type=F32)
    out = o_ref[...] + 0.5 * y
    if has_final:
        out = _rms(out, fg_ref[...])
    o_ref[...] = out


def _ffn(x, mixes, g, win3, wdown, final_g=None):
    rows, d = x.shape
    n_chunks, _, two_chunk = win3.shape
    d_ff = wdown.shape[0]
    tm = ROW_TILE
    n_mix = len(mixes)
    has_final = final_g is not None

    row_spec = lambda width: pl.BlockSpec((tm, width), lambda i: (i, 0))
    in_specs = [row_spec(d)]
    in_specs += [row_spec(a.shape[1]) for a, _ in mixes]
    in_specs += [_resident(w.shape) for _, w in mixes]
    in_specs += [_resident((1, d)), _resident(win3.shape), _resident(wdown.shape)]
    args = [x] + [a for a, _ in mixes] + [w for _, w in mixes] + [g.reshape(1, d), win3, wdown]
    if has_final:
        in_specs.append(_resident((1, d)))
        args.append(final_g.reshape(1, d))

    est = (4 * tm * d * 4
           + sum(2 * tm * a.shape[1] * 2 + w.size * 2 for a, w in mixes)
           + (win3.size + wdown.size) * 2
           + tm * d * 2 + tm * d_ff * 2
           + tm * two_chunk * 4 * 2 + tm * d * 4 * 2)
    return pl.pallas_call(
        functools.partial(_ffn_body, n_mix, n_chunks, has_final),
        grid=(rows // tm,),
        in_specs=in_specs,
        out_specs=row_spec(d),
        out_shape=jax.ShapeDtypeStruct((rows, d), F32),
        scratch_shapes=[pltpu.VMEM((tm, d), BF16), pltpu.VMEM((tm, d_ff), BF16)],
        compiler_params=pltpu.CompilerParams(
            dimension_semantics=("arbitrary",), vmem_limit_bytes=_vmem_limit(est)),
        name="ffn_half_step",
    )(*args)


def _even_proj_body(width, heads, x_ref, g_ref, w_ref, bf_ref, q_ref, k_ref, v_ref, u_ref, lf_ref, lfw_ref):
    h = _rms(x_ref[...], g_ref[...]).astype(BF16)
    z = jnp.dot(h, w_ref[...], preferred_element_type=F32)
    q_ref[...] = (z[:, :width] * (FOX_HEAD_DIM ** -0.5 * LOG2E)).astype(BF16)
    k_ref[...] = z[:, width:2 * width]
    v_ref[...] = z[:, 2 * width:3 * width]
    u_ref[...] = z[:, 3 * width:4 * width]
    f = z[:, 4 * width:4 * width + V7X_LANES] + bf_ref[...]
    logf = jnp.minimum(f, 0.0) - jnp.log(1.0 + jnp.exp(-jnp.abs(f)))
    lf_ref[...] = logf[:, :heads]
    lfw_ref[...] = logf


def _gate_lanes(heads):
    return [heads + GATE_COPIES * h + j for h in range(heads) for j in range(GATE_COPIES)], \
           [h for h in range(heads) for _ in range(GATE_COPIES)]


def _replicate_gates(cols, heads):
    lanes, src = _gate_lanes(heads)
    pad = V7X_LANES - heads - len(lanes)
    return jnp.concatenate([cols, jnp.take(cols, jnp.array(src), axis=-1),
                            jnp.zeros(cols.shape[:-1] + (pad,), cols.dtype)], axis=-1)


def _even_proj(x, g, w_all, bf_pad, width, heads):
    rows, d = x.shape
    tm = ROW_TILE
    n_out = w_all.shape[1]
    row_spec = lambda wd: pl.BlockSpec((tm, wd), lambda i: (i, 0))
    est = (2 * tm * d * 4 + w_all.size * 2 + 2 * tm * width * (2 + 3 * 4)
           + 4 * tm * V7X_LANES * 4 + tm * n_out * 4 * 2 + tm * d * 2)
    return pl.pallas_call(
        functools.partial(_even_proj_body, width, heads),
        grid=(rows // tm,),
        in_specs=[row_spec(d), _resident((1, d)), _resident(w_all.shape), _resident((1, V7X_LANES))],
        out_specs=[row_spec(width), row_spec(width), row_spec(width), row_spec(width), row_spec(heads),
                   row_spec(V7X_LANES)],
        out_shape=[jax.ShapeDtypeStruct((rows, width), BF16),
                   jax.ShapeDtypeStruct((rows, width), F32),
                   jax.ShapeDtypeStruct((rows, width), F32),
                   jax.ShapeDtypeStruct((rows, width), F32),
                   jax.ShapeDtypeStruct((rows, heads), F32),
                   jax.ShapeDtypeStruct((rows, V7X_LANES), F32)],
        compiler_params=pltpu.CompilerParams(
            dimension_semantics=("arbitrary",), vmem_limit_bytes=_vmem_limit(est)),
        name="even_projection",
    )(x, g.reshape(1, d), w_all, bf_pad)


def _attn_body(past, t_new, tq, heads, *refs):
    n_all = past + t_new
    n_pad = _round_up(n_all, V7X_LANES)
    if past:
        (q_ref, k_ref, v_ref, lf_ref, kp_ref, vp_ref, lfp_ref, u_ref, hist_ref, pw_ref, ps_ref,
         att_ref, pool_ref, ca, cb, c_hi, c_mid, c_lo, kcat, qcat, vcat, pa, pb) = refs
    else:
        (q_ref, k_ref, v_ref, lf_ref, u_ref, hist_ref, pw_ref, ps_ref,
         att_ref, pool_ref, ca, cb, c_hi, c_mid, c_lo, kcat, qcat, vcat, pa, pb) = refs
    pair = pl.program_id(1)

    @pl.when(pair == 0)
    def _():
        pad = 8
        ca[0:pad, :] = jnp.zeros((pad, V7X_LANES), F32)
        cb[0:pad, :] = jnp.zeros((pad, V7X_LANES), F32)
        if past:
            ca[pad:pad + past, :] = lfp_ref[...]
        ca[pad + past:pad + n_all, :] = lf_ref[...]
        src, dst = ca, cb
        s = 1
        while s < n_all:
            if s < pad:
                dst[pad:pad + n_all, :] = src[pad:pad + n_all, :] + src[pad - s:pad + n_all - s, :]
            else:
                dst[pad:pad + s, :] = src[pad:pad + s, :]
                dst[pad + s:pad + n_all, :] = src[pad + s:pad + n_all, :] + src[pad:pad + n_all - s, :]
            src, dst = dst, src
            s *= 2
        c = src[pad:pad + n_all, :] * LOG2E
        hi = c.astype(BF16)
        r1 = c - hi.astype(F32)
        mid = r1.astype(BF16)
        c_hi[...] = hi
        c_mid[...] = mid
        c_lo[...] = (r1 - mid.astype(F32)).astype(BF16)
        vcat[0:n_all, V7X_LANES:2 * V7X_LANES] = jnp.ones((n_all, V7X_LANES), BF16)
        if n_pad > n_all:
            kcat[n_all:n_pad, :] = jnp.zeros((n_pad - n_all, V7X_LANES + AUG_LANES), BF16)
            vcat[n_all:n_pad, :] = jnp.zeros((n_pad - n_all, 2 * V7X_LANES), BF16)

    ln = lax.broadcasted_iota(jnp.int32, (1, AUG_LANES), 1)
    base = heads + 2 * GATE_COPIES * pair

    def pat(lanes, value=1.0):
        out = jnp.zeros((1, AUG_LANES), F32)
        for l in lanes:
            out = jnp.where(ln == base + l, value, out)
        return out.astype(BF16)

    k_lanes = lambda j: [hh * GATE_COPIES + N_SPLIT + j for hh in range(2)]
    aug_k = (c_hi[...] * pat(k_lanes(0), -1.0) + c_mid[...] * pat(k_lanes(1), -1.0)
             + c_lo[...] * pat(k_lanes(2), -1.0)
             + pat([hh * GATE_COPIES + j for hh in range(2) for j in range(N_SPLIT)]))
    if past:
        kcat[0:past, 0:V7X_LANES] = kp_ref[...].astype(BF16)
        vcat[0:past, 0:V7X_LANES] = vp_ref[...].astype(BF16)
    kcat[past:n_all, 0:V7X_LANES] = k_ref[...].astype(BF16)
    vcat[past:n_all, 0:V7X_LANES] = v_ref[...].astype(BF16)
    kcat[0:n_all, V7X_LANES:V7X_LANES + AUG_LANES] = aug_k

    qv = q_ref[...]
    hi_q, mid_q, lo_q = c_hi[past:n_all, :], c_mid[past:n_all, :], c_lo[past:n_all, :]
    for hh in range(2):
        lo_lane = hh * GATE_COPIES
        aug_q = (hi_q * pat([lo_lane]) + mid_q * pat([lo_lane + 1]) + lo_q * pat([lo_lane + 2])
                 + pat(range(lo_lane + N_SPLIT, lo_lane + GATE_COPIES)))
        in_head = (ln >= hh * FOX_HEAD_DIM) & (ln < (hh + 1) * FOX_HEAD_DIM)
        q_h = qv * jnp.where(in_head, 1.0, 0.0).astype(BF16)
        for i in range(t_new // tq):
            qcat[i, hh * tq:(hh + 1) * tq, 0:V7X_LANES] = q_h[i * tq:(i + 1) * tq]
            qcat[i, hh * tq:(hh + 1) * tq, V7X_LANES:V7X_LANES + AUG_LANES] = aug_q[i * tq:(i + 1) * tq]

    nt = (((1,), (1,)), ((), ()))
    lane_o = lax.broadcasted_iota(jnp.int32, (tq, V7X_LANES), 1)
    for i in range(t_new // tq):
        r0 = i * tq
        kend = _round_up(past + r0 + tq, V7X_LANES)
        d0 = (past + r0) // V7X_LANES * V7X_LANES
        qs = qcat[i]
        bounds = [(c0, min(c0 + ATT_KV_CHUNK, kend)) for c0 in range(0, kend, ATT_KV_CHUNK)]
        s_list = []
        mx = None
        for c0, c1 in bounds:
            s = lax.dot_general(qs, kcat[c0:c1, :], nt, preferred_element_type=F32)
            if c1 > d0:
                qpos = jnp.bitwise_and(lax.broadcasted_iota(jnp.int32, (2 * tq, c1 - c0), 0), tq - 1) + (past + r0)
                kpos = lax.broadcasted_iota(jnp.int32, (2 * tq, c1 - c0), 1) + c0
                s = jnp.where(kpos <= qpos, s, -jnp.inf)
            s_list.append(s)
            fold = s[:, 0:V7X_LANES]
            for l0 in range(V7X_LANES, c1 - c0, V7X_LANES):
                fold = jnp.maximum(fold, s[:, l0:l0 + V7X_LANES])
            mx = fold if mx is None else jnp.maximum(mx, fold)
        m = jnp.max(mx, axis=1, keepdims=True)
        o = None
        for s, (c0, c1) in zip(s_list, bounds):
            pv = jnp.dot(jnp.exp2(s - m).astype(BF16), vcat[c0:c1, :], preferred_element_type=F32)
            o = pv if o is None else o + pv
        o = o[:, 0:V7X_LANES] / o[:, V7X_LANES:2 * V7X_LANES]
        att_ref[r0:r0 + tq, :] = jnp.where(lane_o < FOX_HEAD_DIM, o[0:tq], o[tq:2 * tq]).astype(BF16)

    lead = 2 * (POOL_HIST + 1)
    half = POOL_HIST + 1
    pa[0:half, :] = jnp.zeros((half, V7X_LANES), F32)
    pb[0:half, :] = jnp.zeros((half, V7X_LANES), F32)
    pa[half:lead, :] = hist_ref[...]
    pa[lead:lead + t_new, :] = u_ref[...]
    src, dst = pa, pb
    win = None
    shift = 1
    for gi in range(len(POOL_WINDOWS)):
        dst[half:lead + t_new, :] = src[half:lead + t_new, :] + src[half - shift:lead + t_new - shift, :]
        cur = dst[lead:lead + t_new, :]
        win = cur if win is None else jnp.where(pair >= gi, cur, win)
        src, dst = dst, src
        shift *= 2
    width = jnp.left_shift(2, pair)
    posn = lax.broadcasted_iota(jnp.int32, (t_new, V7X_LANES), 0) + past
    cnt = jnp.minimum(width, posn + 1).astype(F32)
    dlt = win / cnt - u_ref[...]
    y = jnp.dot(dlt.astype(BF16), pw_ref[...], preferred_element_type=F32)
    pool_ref[...] = (y * ps_ref[...]).astype(BF16)


def _attention_pool(heads, q, k, v, logf_wide, u, hist16, pool_w, pool_scale, past_kvf=None):
    b, t_new, width = q.shape
    n_pairs = width // V7X_LANES
    assert heads + heads * GATE_COPIES <= AUG_LANES
    past = 0 if past_kvf is None else past_kvf[0].shape[1]
    n_all = past + t_new
    n_pad = _round_up(n_all, V7X_LANES)
    tq = min(ATT_Q_BLOCK, t_new)
    assert tq & (tq - 1) == 0 and t_new % tq == 0 and heads % 2 == 0

    slab = lambda rows: pl.BlockSpec((None, rows, V7X_LANES), lambda bi, pi: (bi, 0, pi))
    gates = lambda rows: pl.BlockSpec((None, rows, V7X_LANES), lambda bi, pi: (bi, 0, 0))
    in_specs = [slab(t_new), slab(t_new), slab(t_new), gates(t_new)]
    args = [q, k, v, logf_wide]
    if past:
        in_specs += [slab(past), slab(past), gates(past)]
        args += list(past_kvf)
    in_specs += [slab(t_new), slab(POOL_HIST + 1),
                 pl.BlockSpec((None, V7X_LANES, V7X_LANES), lambda bi, pi: (pi, 0, 0)),
                 pl.BlockSpec((None, 1, V7X_LANES), lambda bi, pi: (pi, 0, 0))]
    args += [u, hist16, pool_w, pool_scale]

    lane_pad_f32 = V7X_LANES * 4
    scratch = [
        pltpu.VMEM((8 + n_all, V7X_LANES), F32), pltpu.VMEM((8 + n_all, V7X_LANES), F32),
        pltpu.VMEM((n_all, AUG_LANES), BF16), pltpu.VMEM((n_all, AUG_LANES), BF16),
        pltpu.VMEM((n_all, AUG_LANES), BF16),
        pltpu.VMEM((n_pad, V7X_LANES + AUG_LANES), BF16),
        pltpu.VMEM((t_new // tq, 2 * tq, V7X_LANES + AUG_LANES), BF16),
        pltpu.VMEM((n_pad, 2 * V7X_LANES), BF16),
        pltpu.VMEM((2 * (POOL_HIST + 1) + t_new, V7X_LANES), F32),
        pltpu.VMEM((2 * (POOL_HIST + 1) + t_new, V7X_LANES), F32),
    ]
    est = (2 * t_new * V7X_LANES * (2 + 4 + 4 + 4 + 2 + 2) + 2 * t_new * lane_pad_f32
           + 2 * past * (2 * V7X_LANES * 4 + lane_pad_f32)
           + 3 * (8 + n_all) * lane_pad_f32 + n_all * lane_pad_f32
           + 2 * n_pad * (V7X_LANES + AUG_LANES) * 2 + 2 * t_new * (V7X_LANES + AUG_LANES) * 2
           + 2 * (32 + t_new) * lane_pad_f32
           + 4 * 2 * tq * n_pad * 4 + 10 * n_all * lane_pad_f32)
    return pl.pallas_call(
        functools.partial(_attn_body, past, t_new, tq, heads),
        grid=(b, n_pairs),
        in_specs=in_specs,
        out_specs=[slab(t_new), slab(t_new)],
        out_shape=[jax.ShapeDtypeStruct((b, t_new, width), BF16),
                   jax.ShapeDtypeStruct((b, t_new, width), BF16)],
        scratch_shapes=scratch,
        compiler_params=pltpu.CompilerParams(
            dimension_semantics=("arbitrary", "arbitrary"), vmem_limit_bytes=_vmem_limit(est)),
        name="fox_attention_pool",
    )(*args)


def _sgu_body(groups, want_zv, x_ref, g_ref, w_ref, ng_ref, ws_ref, bias_ref, *rest):
    if want_zv:
        o_ref, zv_ref, zvb_scr = rest
    else:
        o_ref, zvb_scr = rest
        zv_ref = None
    tm = x_ref.shape[0]
    width = ng_ref.shape[1]
    gdim = width // groups
    h = _rms(x_ref[...], g_ref[...]).astype(BF16)
    z = jnp.dot(h, w_ref[...], preferred_element_type=F32)
    z = 0.5 * z * (1.0 + jnp.tanh(math.sqrt(2.0 / math.pi) * (z + 0.044715 * (z * z * z))))
    zv = _rms(z[:, width:], ng_ref[...])
    if want_zv:
        zv_ref[...] = zv
    zvb_scr[...] = zv.astype(BF16)
    for ch in range(tm // SGU_CHUNK):
        rs = slice(ch * SGU_CHUNK, (ch + 1) * SGU_CHUNK)
        for gi in range(groups):
            cs = slice(gi * gdim, (gi + 1) * gdim)
            mix = jnp.dot(ws_ref[gi], zvb_scr[rs, cs], preferred_element_type=F32) + bias_ref[:, cs]
            o_ref[rs, cs] = (z[rs, cs] * mix).astype(BF16)


def _sgu(x, g, w_in, norm_g, ws_mat, bias_full, want_zv):
    rows, d = x.shape
    width = norm_g.shape[0]
    groups = ws_mat.shape[0]
    tm = ROW_TILE
    row_spec = lambda wd: pl.BlockSpec((tm, wd), lambda i: (i, 0))
    out_specs = [row_spec(width)]
    out_shape = [jax.ShapeDtypeStruct((rows, width), BF16)]
    if want_zv:
        out_specs.append(row_spec(width))
        out_shape.append(jax.ShapeDtypeStruct((rows, width), F32))
    est = (2 * tm * d * 4 + w_in.size * 2 + ws_mat.size * 2 + bias_full.size * 4
           + 2 * tm * width * 2 + (2 * tm * width * 4 if want_zv else 0)
           + tm * width * 2 + 3 * tm * 2 * width * 4)
    res = pl.pallas_call(
        functools.partial(_sgu_body, groups, want_zv),
        grid=(rows // tm,),
        in_specs=[row_spec(d), _resident((1, d)), _resident(w_in.shape), _resident((1, width)),
                  _resident(ws_mat.shape), _resident(bias_full.shape)],
        out_specs=out_specs,
        out_shape=out_shape,
        scratch_shapes=[pltpu.VMEM((tm, width), BF16)],
        compiler_params=pltpu.CompilerParams(
            dimension_semantics=("arbitrary",), vmem_limit_bytes=_vmem_limit(est)),
        name="spatial_gating",
    )(x, g.reshape(1, d), w_in, norm_g.reshape(1, width), ws_mat, bias_full)
    return res if want_zv else (res[0], None)


def _prep_ffn(w_in, w_down):
    d, two_ff = w_in.shape
    d_ff = two_ff // 2
    n_chunks = d_ff // FF_CHUNK
    gate = w_in[:, :d_ff].reshape(d, n_chunks, FF_CHUNK)
    up = w_in[:, d_ff:].reshape(d, n_chunks, FF_CHUNK)
    win3 = jnp.concatenate([gate, up], axis=-1).transpose(1, 0, 2).astype(BF16)
    return win3, w_down.astype(BF16)


def _sgu_spatial(w_s, b_s, seq_len):
    groups = w_s.shape[0]
    span = min(seq_len, SGU_CHUNK)
    idx = jnp.arange(span) // SGU_STREAM_CHUNK
    w = jnp.where((idx[None, :] <= idx[:, None])[None], w_s[:, :span, :span], 0.0)
    reps = SGU_CHUNK // span
    if reps > 1:
        w = jnp.einsum('ab,gts->gatbs', jnp.eye(reps, dtype=w.dtype), w).reshape(groups, SGU_CHUNK, SGU_CHUNK)
    bias = jnp.tile(b_s[:, :span], (1, reps))
    return w.astype(BF16), bias


def kernel(x_prompt, x_sample, cache_k, cache_v, cache_logf, state_pool, norm_g, ffn_w_in, ffn_w_down,
           even_w_in, even_b_f, pool_w, pool_scale, even_w_out, sgu_w_in, sgu_norm_g, sgu_w_s, sgu_b_s,
           sgu_w_out, final_g):
    b, s, d = x_prompt.shape
    bs, t, _ = x_sample.shape
    depth = norm_g.shape[0]
    heads = even_b_f.shape[1]
    width = heads * FOX_HEAD_DIM
    pool_width = pool_scale.shape[1]
    n_groups = pool_w.shape[1]
    past = cache_k.shape[2]
    sgu_width = sgu_norm_g.shape[1]
    sgu_groups = sgu_w_s.shape[1]
    sgu_gdim = sgu_width // sgu_groups

    xp = x_prompt.reshape(b * s, d)
    xs = x_sample.reshape(bs * t, d)
    kp_l, vp_l, fp_l, up_l = [], [], [], []
    ks_l, vs_l, fs_l, us_l, zs_l = [], [], [], [], []

    for l in range(depth):
        win_a, wdown_a = _prep_ffn(ffn_w_in[l, 0], ffn_w_down[l, 0])
        win_b, wdown_b = _prep_ffn(ffn_w_in[l, 1], ffn_w_down[l, 1])
        last = final_g if l == depth - 1 else None
        xp = _ffn(xp, [], norm_g[l, 0], win_a, wdown_a)
        xs = _ffn(xs, [], norm_g[l, 0], win_a, wdown_a)
        if l % 2 == 0:
            e = l // 2
            w_e = even_w_in[e]
            f_cols = _replicate_gates(w_e[:, 3 * width:3 * width + heads], heads)
            w_all = jnp.concatenate([w_e[:, :3 * width], w_e[:, 3 * width + heads:], f_cols], axis=1).astype(BF16)
            bf_pad = _replicate_gates(even_b_f[e], heads).reshape(1, V7X_LANES)
            pw = pool_w[e].astype(BF16)
            psc = pool_scale[e].reshape(n_groups, 1, pool_width // n_groups)
            w_out_att = even_w_out[e, :width].astype(BF16)
            w_out_pool = even_w_out[e, width:].astype(BF16)

            q, k, v, u, lf, lfw = _even_proj(xp, norm_g[l, 1], w_all, bf_pad, width, heads)
            hist = jnp.zeros((b, POOL_HIST + 1, pool_width), F32)
            att, pool = _attention_pool(heads, q.reshape(b, s, width), k.reshape(b, s, width),
                                        v.reshape(b, s, width), lfw.reshape(b, s, V7X_LANES),
                                        u.reshape(b, s, pool_width), hist, pw, psc)
            xp = _ffn(xp, [(att.reshape(b * s, width), w_out_att), (pool.reshape(b * s, pool_width), w_out_pool)],
                      norm_g[l, 2], win_b, wdown_b, last)
            kp_l.append(k.reshape(b, s, heads, FOX_HEAD_DIM))
            vp_l.append(v.reshape(b, s, heads, FOX_HEAD_DIM))
            fp_l.append(lf.reshape(b, s, heads))
            up_l.append(u.reshape(b, s, pool_width)[:, s - POOL_HIST:])

            q, k, v, u, lf, lfw = _even_proj(xs, norm_g[l, 1], w_all, bf_pad, width, heads)
            hist = jnp.pad(state_pool[e], ((0, 0), (1, 0), (0, 0)))
            past_kvf = (cache_k[e].reshape(bs, past, width), cache_v[e].reshape(bs, past, width),
                        _replicate_gates(cache_logf[e], heads))
            att, pool = _attention_pool(heads, q.reshape(bs, t, width), k.reshape(bs, t, width),
                                        v.reshape(bs, t, width), lfw.reshape(bs, t, V7X_LANES),
                                        u.reshape(bs, t, pool_width), hist, pw, psc, past_kvf)
            xs = _ffn(xs, [(att.reshape(bs * t, width), w_out_att), (pool.reshape(bs * t, pool_width), w_out_pool)],
                      norm_g[l, 2], win_b, wdown_b, last)
            ks_l.append(k.reshape(bs, t, heads, FOX_HEAD_DIM))
            vs_l.append(v.reshape(bs, t, heads, FOX_HEAD_DIM))
            fs_l.append(lf.reshape(bs, t, heads))
            u_ext = jnp.concatenate([state_pool[e], u.reshape(bs, t, pool_width)], axis=1)
            us_l.append(u_ext[:, -POOL_HIST:])
        else:
            o = l // 2
            w_in = sgu_w_in[o].astype(BF16)
            w_out = sgu_w_out[o].astype(BF16)
            ws_p, bias_p = _sgu_spatial(sgu_w_s[o], sgu_b_s[o], s)
            ws_s, bias_s = _sgu_spatial(sgu_w_s[o], sgu_b_s[o], t)
            expand = lambda bias: jnp.repeat(bias.T, sgu_gdim, axis=1)
            gated, _ = _sgu(xp, norm_g[l, 1], w_in, sgu_norm_g[o], ws_p, expand(bias_p), False)
            xp = _ffn(xp, [(gated, w_out)], norm_g[l, 2], win_b, wdown_b, last)
            gated, zv = _sgu(xs, norm_g[l, 1], w_in, sgu_norm_g[o], ws_s, expand(bias_s), True)
            xs = _ffn(xs, [(gated, w_out)], norm_g[l, 2], win_b, wdown_b, last)
            zs_l.append(zv.reshape(bs, t, sgu_width))

    return (xp.reshape(b, s, d), xs.reshape(bs, t, d),
            jnp.stack(kp_l), jnp.stack(vp_l), jnp.stack(fp_l), jnp.stack(up_l),
            jnp.stack(ks_l), jnp.stack(vs_l), jnp.stack(fs_l), jnp.stack(us_l), jnp.stack(zs_l))
```

```python
import functools
import math

import jax
import jax.numpy as jnp
from jax import lax
from jax.experimental import pallas as pl
from jax.experimental.pallas import tpu as pltpu

F32 = jnp.float32
BF16 = jnp.bfloat16

EPS = 1e-6
FOX_HEAD_DIM = 64
POOL_WINDOWS = (2, 4, 8, 16)
POOL_HIST = max(POOL_WINDOWS) - 1
SGU_STREAM_CHUNK = 64
SGU_CHUNK = 128

V7X_LANES = 128
V7X_MXU_DIM = 256
V7X_VMEM_BYTES = 64 * 1024 * 1024
VMEM_COMPILER_RESERVE = 8 * 1024 * 1024

ROW_TILE = 512
FF_CHUNK = V7X_MXU_DIM
ATT_Q_BLOCK = 128
ATT_KV_CHUNK = V7X_MXU_DIM
AUG_LANES = V7X_LANES
N_SPLIT = 3
GATE_COPIES = 2 * N_SPLIT
LOG2E = math.log2(math.e)
GELU_A = math.sqrt(2.0 / math.pi)
GELU_B = 0.044715


def _vmem_limit(est_bytes):
    return int(min(max(est_bytes, 16 * 1024 * 1024), V7X_VMEM_BYTES - VMEM_COMPILER_RESERVE))


def _resident(shape):
    nd = len(shape)
    return pl.BlockSpec(shape, lambda *_: (0,) * nd, pipeline_mode=pl.Buffered(1))


def _rms(x, g):
    ms = jnp.mean(x * x, axis=-1, keepdims=True)
    return x * lax.rsqrt(ms + EPS) * g


def _round_up(n, m):
    return (n + m - 1) // m * m


def _ffn_body(n_mix, n_chunks, has_final, *refs):
    x_ref = refs[0]
    a_refs = refs[1:1 + n_mix]
    w_refs = refs[1 + n_mix:1 + 2 * n_mix]
    g_ref, win_ref, wdown_ref = refs[1 + 2 * n_mix:4 + 2 * n_mix]
    pos = 4 + 2 * n_mix
    fg_ref = refs[pos] if has_final else None
    pos += int(has_final)
    o_ref, h_scr, act_scr = refs[pos:pos + 3]

    x = x_ref[...]
    for a_ref, w_ref in zip(a_refs, w_refs):
        x = x + jnp.dot(a_ref[...], w_ref[...], preferred_element_type=F32)
    o_ref[...] = x
    h_scr[...] = _rms(x, g_ref[...]).astype(BF16)
    d_ff = n_chunks * FF_CHUNK
    for j in range(n_chunks):
        cols = slice(j * FF_CHUNK, (j + 1) * FF_CHUNK)
        up_cols = slice(d_ff + j * FF_CHUNK, d_ff + (j + 1) * FF_CHUNK)
        gate = jnp.dot(h_scr[...], win_ref[:, cols], preferred_element_type=F32)
        up = jnp.dot(h_scr[...], win_ref[:, up_cols], preferred_element_type=F32)
        act = gate * (1.0 / (1.0 + jnp.exp(-gate))) * up
        act_scr[:, cols] = act.astype(BF16)
    y = jnp.dot(act_scr[...], wdown_ref[...], preferred_element_type=F32)
    out = o_ref[...] + 0.5 * y
    if has_final:
        out = _rms(out, fg_ref[...])
    o_ref[...] = out


def _ffn(x, mixes, g, win3, wdown, final_g=None):
    rows, d = x.shape
    d_ff = wdown.shape[0]
    assert win3.shape == (d, 2 * d_ff) and d_ff % FF_CHUNK == 0
    n_chunks = d_ff // FF_CHUNK
    two_chunk = 2 * FF_CHUNK
    tm = ROW_TILE
    n_mix = len(mixes)
    has_final = final_g is not None

    row_spec = lambda width: pl.BlockSpec((tm, width), lambda i: (i, 0))
    in_specs = [row_spec(d)]
    in_specs += [row_spec(a.shape[1]) for a, _ in mixes]
    in_specs += [_resident(w.shape) for _, w in mixes]
    in_specs += [_resident((1, d)), _resident(win3.shape), _resident(wdown.shape)]
    args = [x] + [a for a, _ in mixes] + [w for _, w in mixes] + [g.reshape(1, d), win3, wdown]
    if has_final:
        in_specs.append(_resident((1, d)))
        args.append(final_g.reshape(1, d))

    est = (4 * tm * d * 4
           + sum(2 * tm * a.shape[1] * 2 + w.size * 2 for a, w in mixes)
           + (win3.size + wdown.size) * 2
           + tm * d * 2 + tm * d_ff * 2
           + tm * two_chunk * 4 * 2 + tm * d * 4 * 2)
    return pl.pallas_call(
        functools.partial(_ffn_body, n_mix, n_chunks, has_final),
        grid=(rows // tm,),
        in_specs=in_specs,
        out_specs=row_spec(d),
        out_shape=jax.ShapeDtypeStruct((rows, d), F32),
        scratch_shapes=[pltpu.VMEM((tm, d), BF16), pltpu.VMEM((tm, d_ff), BF16)],
        compiler_params=pltpu.CompilerParams(
            dimension_semantics=("arbitrary",), vmem_limit_bytes=_vmem_limit(est)),
        name="ffn_half_step",
    )(*args)


def _even_proj_body(width, heads, x_ref, g_ref, w_ref, bf_ref,
                    q_ref, kb_ref, vb_ref, k4_ref, v4_ref, u_ref, lf_ref, lfw_ref):
    tm = x_ref.shape[0]
    h = _rms(x_ref[...], g_ref[...]).astype(BF16)
    z = jnp.dot(h, w_ref[...], preferred_element_type=F32)
    q_ref[...] = (z[:, :width] * (FOX_HEAD_DIM ** -0.5 * LOG2E)).astype(BF16)
    kb_ref[...] = z[:, width:2 * width].astype(BF16)
    vb_ref[...] = z[:, 2 * width:3 * width].astype(BF16)
    for hd in range(heads):
        k4_ref[pl.ds(hd, tm, stride=heads), :] = z[:, width + hd * FOX_HEAD_DIM:width + (hd + 1) * FOX_HEAD_DIM]
        v4_ref[pl.ds(hd, tm, stride=heads), :] = z[:, 2 * width + hd * FOX_HEAD_DIM:
                                                   2 * width + (hd + 1) * FOX_HEAD_DIM]
    u_ref[...] = z[:, 3 * width:4 * width]
    f = z[:, 4 * width:4 * width + V7X_LANES] + bf_ref[...]
    logf = jnp.minimum(f, 0.0) - jnp.log(1.0 + jnp.exp(-jnp.abs(f)))
    lf_ref[...] = logf[:, :heads]
    lfw_ref[...] = logf


def _gate_lanes(heads):
    return [heads + GATE_COPIES * h + j for h in range(heads) for j in range(GATE_COPIES)], \
           [h for h in range(heads) for _ in range(GATE_COPIES)]


def _replicate_gates(cols, heads):
    lanes, src = _gate_lanes(heads)
    pad = V7X_LANES - heads - len(lanes)
    return jnp.concatenate([cols, jnp.take(cols, jnp.array(src), axis=-1),
                            jnp.zeros(cols.shape[:-1] + (pad,), cols.dtype)], axis=-1)


def _even_proj(x, g, w_all, bf_pad, width, heads):
    rows, d = x.shape
    tm = ROW_TILE
    n_out = w_all.shape[1]
    row_spec = lambda wd: pl.BlockSpec((tm, wd), lambda i: (i, 0))
    head_rows = pl.BlockSpec((tm * heads, FOX_HEAD_DIM), lambda i: (i, 0))
    est = (2 * tm * d * 4 + w_all.size * 2 + 2 * tm * width * (3 * 2 + 4)
           + 2 * 2 * tm * heads * V7X_LANES * 4
           + 4 * tm * V7X_LANES * 4 + tm * n_out * 4 * 2 + tm * d * 2)
    return pl.pallas_call(
        functools.partial(_even_proj_body, width, heads),
        grid=(rows // tm,),
        in_specs=[row_spec(d), _resident((1, d)), _resident(w_all.shape), _resident((1, V7X_LANES))],
        out_specs=[row_spec(width), row_spec(width), row_spec(width), head_rows, head_rows,
                   row_spec(width), row_spec(heads), row_spec(V7X_LANES)],
        out_shape=[jax.ShapeDtypeStruct((rows, width), BF16),
                   jax.ShapeDtypeStruct((rows, width), BF16),
                   jax.ShapeDtypeStruct((rows, width), BF16),
                   jax.ShapeDtypeStruct((rows * heads, FOX_HEAD_DIM), F32),
                   jax.ShapeDtypeStruct((rows * heads, FOX_HEAD_DIM), F32),
                   jax.ShapeDtypeStruct((rows, width), F32),
                   jax.ShapeDtypeStruct((rows, heads), F32),
                   jax.ShapeDtypeStruct((rows, V7X_LANES), F32)],
        compiler_params=pltpu.CompilerParams(
            dimension_semantics=("arbitrary",), vmem_limit_bytes=_vmem_limit(est)),
        name="even_projection",
    )(x, g.reshape(1, d), w_all, bf_pad)


def _attn_body(past, t_new, tq, heads, *refs):
    n_all = past + t_new
    n_pad = _round_up(n_all, V7X_LANES)
    if past:
        (q_ref, k_ref, v_ref, lf_ref, kp_ref, vp_ref, lfp_ref, u_ref, hist_ref, pw_ref, ps_ref,
         att_ref, pool_ref, ca, cb, c_hi, c_mid, c_lo, kcat, qcat, vcat, pa, pb) = refs
    else:
        (q_ref, k_ref, v_ref, lf_ref, u_ref, hist_ref, pw_ref, ps_ref,
         att_ref, pool_ref, ca, cb, c_hi, c_mid, c_lo, kcat, qcat, vcat, pa, pb) = refs
    pair = pl.program_id(1)

    @pl.when(pair == 0)
    def _():
        pad = 8
        ca[0:pad, :] = jnp.zeros((pad, V7X_LANES), F32)
        cb[0:pad, :] = jnp.zeros((pad, V7X_LANES), F32)
        if past:
            ca[pad:pad + past, :] = lfp_ref[...]
        ca[pad + past:pad + n_all, :] = lf_ref[...]
        src, dst = ca, cb
        s = 1
        while s < n_all:
            if s < pad:
                dst[pad:pad + n_all, :] = src[pad:pad + n_all, :] + src[pad - s:pad + n_all - s, :]
            else:
                dst[pad:pad + s, :] = src[pad:pad + s, :]
                dst[pad + s:pad + n_all, :] = src[pad + s:pad + n_all, :] + src[pad:pad + n_all - s, :]
            src, dst = dst, src
            s *= 2
        c = src[pad:pad + n_all, :] * LOG2E
        hi = c.astype(BF16)
        r1 = c - hi.astype(F32)
        mid = r1.astype(BF16)
        c_hi[...] = hi
        c_mid[...] = mid
        c_lo[...] = (r1 - mid.astype(F32)).astype(BF16)
        vcat[0:n_all, V7X_LANES:2 * V7X_LANES] = jnp.ones((n_all, V7X_LANES), BF16)
        if n_pad > n_all:
            kcat[n_all:n_pad, :] = jnp.zeros((n_pad - n_all, V7X_LANES + AUG_LANES), BF16)
            vcat[n_all:n_pad, :] = jnp.zeros((n_pad - n_all, 2 * V7X_LANES), BF16)

    ln = lax.broadcasted_iota(jnp.int32, (1, AUG_LANES), 1)
    base = heads + 2 * GATE_COPIES * pair

    def pat(lanes, value=1.0):
        out = jnp.zeros((1, AUG_LANES), F32)
        for l in lanes:
            out = jnp.where(ln == base + l, value, out)
        return out.astype(BF16)

    k_lanes = lambda j: [hh * GATE_COPIES + N_SPLIT + j for hh in range(2)]
    aug_k = (c_hi[...] * pat(k_lanes(0), -1.0) + c_mid[...] * pat(k_lanes(1), -1.0)
             + c_lo[...] * pat(k_lanes(2), -1.0)
             + pat([hh * GATE_COPIES + j for hh in range(2) for j in range(N_SPLIT)]))
    if past:
        kcat[0:past, 0:V7X_LANES] = kp_ref[...].astype(BF16)
        vcat[0:past, 0:V7X_LANES] = vp_ref[...].astype(BF16)
    kcat[past:n_all, 0:V7X_LANES] = k_ref[...]
    vcat[past:n_all, 0:V7X_LANES] = v_ref[...]
    kcat[0:n_all, V7X_LANES:V7X_LANES + AUG_LANES] = aug_k

    qv = q_ref[...]
    hi_q, mid_q, lo_q = c_hi[past:n_all, :], c_mid[past:n_all, :], c_lo[past:n_all, :]
    for hh in range(2):
        lo_lane = hh * GATE_COPIES
        aug_q = (hi_q * pat([lo_lane]) + mid_q * pat([lo_lane + 1]) + lo_q * pat([lo_lane + 2])
                 + pat(range(lo_lane + N_SPLIT, lo_lane + GATE_COPIES)))
        in_head = (ln >= hh * FOX_HEAD_DIM) & (ln < (hh + 1) * FOX_HEAD_DIM)
        q_h = qv * jnp.where(in_head, 1.0, 0.0).astype(BF16)
        for i in range(t_new // tq):
            qcat[i, hh * tq:(hh + 1) * tq, 0:V7X_LANES] = q_h[i * tq:(i + 1) * tq]
            qcat[i, hh * tq:(hh + 1) * tq, V7X_LANES:V7X_LANES + AUG_LANES] = aug_q[i * tq:(i + 1) * tq]

    nt = (((1,), (1,)), ((), ()))
    lane_o = lax.broadcasted_iota(jnp.int32, (tq, V7X_LANES), 1)
    for i in range(t_new // tq):
        r0 = i * tq
        kend = _round_up(past + r0 + tq, V7X_LANES)
        d0 = (past + r0) // V7X_LANES * V7X_LANES
        qs = qcat[i]
        bounds = [(c0, min(c0 + ATT_KV_CHUNK, kend)) for c0 in range(0, kend, ATT_KV_CHUNK)]
        s_list = []
        mx = None
        for c0, c1 in bounds:
            s = lax.dot_general(qs, kcat[c0:c1, :], nt, preferred_element_type=F32)
            if c1 > d0:
                qpos = jnp.bitwise_and(lax.broadcasted_iota(jnp.int32, (2 * tq, c1 - c0), 0), tq - 1) + (past + r0)
                kpos = lax.broadcasted_iota(jnp.int32, (2 * tq, c1 - c0), 1) + c0
                s = jnp.where(kpos <= qpos, s, -jnp.inf)
            s_list.append(s)
            fold = s[:, 0:V7X_LANES]
            for l0 in range(V7X_LANES, c1 - c0, V7X_LANES):
                fold = jnp.maximum(fold, s[:, l0:l0 + V7X_LANES])
            mx = fold if mx is None else jnp.maximum(mx, fold)
        m = jnp.max(mx, axis=1, keepdims=True)
        o = None
        for s, (c0, c1) in zip(s_list, bounds):
            pv = jnp.dot(jnp.exp2(s - m).astype(BF16), vcat[c0:c1, :], preferred_element_type=F32)
            o = pv if o is None else o + pv
        o = o[:, 0:V7X_LANES] / o[:, V7X_LANES:2 * V7X_LANES]
        att_ref[r0:r0 + tq, :] = jnp.where(lane_o < FOX_HEAD_DIM, o[0:tq], o[tq:2 * tq]).astype(BF16)

    lead = 2 * (POOL_HIST + 1)
    half = POOL_HIST + 1
    pa[0:half, :] = jnp.zeros((half, V7X_LANES), F32)
    pb[0:half, :] = jnp.zeros((half, V7X_LANES), F32)
    pa[half:lead, :] = hist_ref[...]
    pa[lead:lead + t_new, :] = u_ref[...]
    src, dst = pa, pb
    win = None
    shift = 1
    for gi in range(len(POOL_WINDOWS)):
        dst[half:lead + t_new, :] = src[half:lead + t_new, :] + src[half - shift:lead + t_new - shift, :]
        cur = dst[lead:lead + t_new, :]
        win = cur if win is None else jnp.where(pair >= gi, cur, win)
        src, dst = dst, src
        shift *= 2
    width = jnp.left_shift(2, pair)
    posn = lax.broadcasted_iota(jnp.int32, (t_new, V7X_LANES), 0) + past
    cnt = jnp.minimum(width, posn + 1).astype(F32)
    dlt = win / cnt - u_ref[...]
    y = jnp.dot(dlt.astype(BF16), pw_ref[...], preferred_element_type=F32)
    pool_ref[...] = (y * ps_ref[...]).astype(BF16)


def _attention_pool(heads, q, k, v, logf_wide, u, hist16, pool_w, pool_scale, past_kvf=None):
    b, t_new, width = q.shape
    n_pairs = width // V7X_LANES
    assert heads + heads * GATE_COPIES <= AUG_LANES
    past = 0 if past_kvf is None else past_kvf[0].shape[1]
    n_all = past + t_new
    n_pad = _round_up(n_all, V7X_LANES)
    tq = min(ATT_Q_BLOCK, t_new)
    assert tq & (tq - 1) == 0 and t_new % tq == 0 and heads % 2 == 0

    slab = lambda rows: pl.BlockSpec((None, rows, V7X_LANES), lambda bi, pi: (bi, 0, pi))
    gates = lambda rows: pl.BlockSpec((None, rows, V7X_LANES), lambda bi, pi: (bi, 0, 0))
    in_specs = [slab(t_new), slab(t_new), slab(t_new), gates(t_new)]
    args = [q, k, v, logf_wide]
    if past:
        in_specs += [slab(past), slab(past), gates(past)]
        args += list(past_kvf)
    in_specs += [slab(t_new), slab(POOL_HIST + 1),
                 pl.BlockSpec((None, V7X_LANES, V7X_LANES), lambda bi, pi: (pi, 0, 0)),
                 pl.BlockSpec((None, 1, V7X_LANES), lambda bi, pi: (pi, 0, 0))]
    args += [u, hist16, pool_w, pool_scale]

    lane_pad_f32 = V7X_LANES * 4
    scratch = [
        pltpu.VMEM((8 + n_all, V7X_LANES), F32), pltpu.VMEM((8 + n_all, V7X_LANES), F32),
        pltpu.VMEM((n_all, AUG_LANES), BF16), pltpu.VMEM((n_all, AUG_LANES), BF16),
        pltpu.VMEM((n_all, AUG_LANES), BF16),
        pltpu.VMEM((n_pad, V7X_LANES + AUG_LANES), BF16),
        pltpu.VMEM((t_new // tq, 2 * tq, V7X_LANES + AUG_LANES), BF16),
        pltpu.VMEM((n_pad, 2 * V7X_LANES), BF16),
        pltpu.VMEM((2 * (POOL_HIST + 1) + t_new, V7X_LANES), F32),
        pltpu.VMEM((2 * (POOL_HIST + 1) + t_new, V7X_LANES), F32),
    ]
    est = (2 * t_new * V7X_LANES * (2 + 2 + 2 + 4 + 2 + 2) + 2 * t_new * lane_pad_f32
           + 2 * past * (2 * V7X_LANES * 4 + lane_pad_f32)
           + 3 * (8 + n_all) * lane_pad_f32 + n_all * lane_pad_f32
           + 2 * n_pad * (V7X_LANES + AUG_LANES) * 2 + 2 * t_new * (V7X_LANES + AUG_LANES) * 2
           + 2 * (32 + t_new) * lane_pad_f32
           + 4 * 2 * tq * n_pad * 4 + 10 * n_all * lane_pad_f32)
    return pl.pallas_call(
        functools.partial(_attn_body, past, t_new, tq, heads),
        grid=(b, n_pairs),
        in_specs=in_specs,
        out_specs=[slab(t_new), slab(t_new)],
        out_shape=[jax.ShapeDtypeStruct((b, t_new, width), BF16),
                   jax.ShapeDtypeStruct((b, t_new, width), BF16)],
        scratch_shapes=scratch,
        compiler_params=pltpu.CompilerParams(
            dimension_semantics=("arbitrary", "arbitrary"), vmem_limit_bytes=_vmem_limit(est)),
        name="fox_attention_pool",
    )(*args)


def _sgu_body(groups, want_zv, x_ref, g_ref, w_ref, ng_ref, ws_ref, bias_ref, *rest):
    if want_zv:
        o_ref, zv_ref, zvb_scr = rest
    else:
        o_ref, zvb_scr = rest
        zv_ref = None
    tm = x_ref.shape[0]
    width = ng_ref.shape[1]
    gdim = width // groups
    h = _rms(x_ref[...], g_ref[...]).astype(BF16)
    z = jnp.dot(h, w_ref[...], preferred_element_type=F32)
    half_z = 0.5 * z
    z = half_z + half_z * jnp.tanh(z * (GELU_A + (GELU_A * GELU_B) * (z * z)))
    zv = _rms(z[:, width:], ng_ref[...])
    if want_zv:
        zv_ref[...] = zv
    zvb_scr[...] = zv.astype(BF16)
    for ch in range(tm // SGU_CHUNK):
        rs = slice(ch * SGU_CHUNK, (ch + 1) * SGU_CHUNK)
        for gi in range(groups):
            cs = slice(gi * gdim, (gi + 1) * gdim)
            mix = jnp.dot(ws_ref[gi], zvb_scr[rs, cs], preferred_element_type=F32) + bias_ref[:, cs]
            o_ref[rs, cs] = (z[rs, cs] * mix).astype(BF16)


def _sgu(x, g, w_in, norm_g, ws_mat, bias_full, want_zv):
    rows, d = x.shape
    width = norm_g.shape[0]
    groups = ws_mat.shape[0]
    tm = ROW_TILE
    row_spec = lambda wd: pl.BlockSpec((tm, wd), lambda i: (i, 0))
    out_specs = [row_spec(width)]
    out_shape = [jax.ShapeDtypeStruct((rows, width), BF16)]
    if want_zv:
        out_specs.append(row_spec(width))
        out_shape.append(jax.ShapeDtypeStruct((rows, width), F32))
    est = (2 * tm * d * 4 + w_in.size * 2 + ws_mat.size * 2 + bias_full.size * 4
           + 2 * tm * width * 2 + (2 * tm * width * 4 if want_zv else 0)
           + tm * width * 2 + 3 * tm * 2 * width * 4)
    res = pl.pallas_call(
        functools.partial(_sgu_body, groups, want_zv),
        grid=(rows // tm,),
        in_specs=[row_spec(d), _resident((1, d)), _resident(w_in.shape), _resident((1, width)),
                  _resident(ws_mat.shape), _resident(bias_full.shape)],
        out_specs=out_specs,
        out_shape=out_shape,
        scratch_shapes=[pltpu.VMEM((tm, width), BF16)],
        compiler_params=pltpu.CompilerParams(
            dimension_semantics=("arbitrary",), vmem_limit_bytes=_vmem_limit(est)),
        name="spatial_gating",
    )(x, g.reshape(1, d), w_in, norm_g.reshape(1, width), ws_mat, bias_full)
    return res if want_zv else (res[0], None)


def _prep_ffn(w_in, w_down):
    return w_in.astype(BF16), w_down.astype(BF16)


def _sgu_spatial(w_s, b_s, seq_len):
    groups = w_s.shape[0]
    span = min(seq_len, SGU_CHUNK)
    idx = jnp.arange(span) // SGU_STREAM_CHUNK
    w = jnp.where((idx[None, :] <= idx[:, None])[None], w_s[:, :span, :span], 0.0)
    reps = SGU_CHUNK // span
    if reps > 1:
        w = jnp.einsum('ab,gts->gatbs', jnp.eye(reps, dtype=w.dtype), w).reshape(groups, SGU_CHUNK, SGU_CHUNK)
    bias = jnp.tile(b_s[:, :span], (1, reps))
    return w.astype(BF16), bias


def kernel(x_prompt, x_sample, cache_k, cache_v, cache_logf, state_pool, norm_g, ffn_w_in, ffn_w_down,
           even_w_in, even_b_f, pool_w, pool_scale, even_w_out, sgu_w_in, sgu_norm_g, sgu_w_s, sgu_b_s,
           sgu_w_out, final_g):
    b, s, d = x_prompt.shape
    bs, t, _ = x_sample.shape
    depth = norm_g.shape[0]
    heads = even_b_f.shape[1]
    width = heads * FOX_HEAD_DIM
    pool_width = pool_scale.shape[1]
    n_groups = pool_w.shape[1]
    past = cache_k.shape[2]
    sgu_width = sgu_norm_g.shape[1]
    sgu_groups = sgu_w_s.shape[1]
    sgu_gdim = sgu_width // sgu_groups

    xp = x_prompt.reshape(b * s, d)
    xs = x_sample.reshape(bs * t, d)
    kp_l, vp_l, fp_l, up_l = [], [], [], []
    ks_l, vs_l, fs_l, us_l, zs_l = [], [], [], [], []

    for l in range(depth):
        win_a, wdown_a = _prep_ffn(ffn_w_in[l, 0], ffn_w_down[l, 0])
        win_b, wdown_b = _prep_ffn(ffn_w_in[l, 1], ffn_w_down[l, 1])
        last = final_g if l == depth - 1 else None
        xp = _ffn(xp, [], norm_g[l, 0], win_a, wdown_a)
        xs = _ffn(xs, [], norm_g[l, 0], win_a, wdown_a)
        if l % 2 == 0:
            e = l // 2
            w_e = even_w_in[e]
            f_cols = _replicate_gates(w_e[:, 3 * width:3 * width + heads], heads)
            w_all = jnp.concatenate([w_e[:, :3 * width], w_e[:, 3 * width + heads:], f_cols], axis=1).astype(BF16)
            bf_pad = _replicate_gates(even_b_f[e], heads).reshape(1, V7X_LANES)
            pw = pool_w[e].astype(BF16)
            psc = pool_scale[e].reshape(n_groups, 1, pool_width // n_groups)
            w_out_att = even_w_out[e, :width].astype(BF16)
            w_out_pool = even_w_out[e, width:].astype(BF16)

            q, kb, vb, k, v, u, lf, lfw = _even_proj(xp, norm_g[l, 1], w_all, bf_pad, width, heads)
            hist = jnp.zeros((b, POOL_HIST + 1, pool_width), F32)
            att, pool = _attention_pool(heads, q.reshape(b, s, width), kb.reshape(b, s, width),
                                        vb.reshape(b, s, width), lfw.reshape(b, s, V7X_LANES),
                                        u.reshape(b, s, pool_width), hist, pw, psc)
            xp = _ffn(xp, [(att.reshape(b * s, width), w_out_att), (pool.reshape(b * s, pool_width), w_out_pool)],
                      norm_g[l, 2], win_b, wdown_b, last)
            kp_l.append(k.reshape(b, s, heads, FOX_HEAD_DIM))
            vp_l.append(v.reshape(b, s, heads, FOX_HEAD_DIM))
            fp_l.append(lf.reshape(b, s, heads))
            up_l.append(u.reshape(b, s, pool_width)[:, s - POOL_HIST:])

            q, kb, vb, k, v, u, lf, lfw = _even_proj(xs, norm_g[l, 1], w_all, bf_pad, width, heads)
            hist = jnp.pad(state_pool[e], ((0, 0), (1, 0), (0, 0)))
            past_kvf = (cache_k[e].reshape(bs, past, width), cache_v[e].reshape(bs, past, width),
                        _replicate_gates(cache_logf[e], heads))
            att, pool = _attention_pool(heads, q.reshape(bs, t, width), kb.reshape(bs, t, width),
                                        vb.reshape(bs, t, width), lfw.reshape(bs, t, V7X_LANES),
                                        u.reshape(bs, t, pool_width), hist, pw, psc, past_kvf)
            xs = _ffn(xs, [(att.reshape(bs * t, width), w_out_att), (pool.reshape(bs * t, pool_width), w_out_pool)],
                      norm_g[l, 2], win_b, wdown_b, last)
            ks_l.append(k.reshape(bs, t, heads, FOX_HEAD_DIM))
            vs_l.append(v.reshape(bs, t, heads, FOX_HEAD_DIM))
            fs_l.append(lf.reshape(bs, t, heads))
            u_ext = jnp.concatenate([state_pool[e], u.reshape(bs, t, pool_width)], axis=1)
            us_l.append(u_ext[:, -POOL_HIST:])
        else:
            o = l // 2
            w_in = sgu_w_in[o].astype(BF16)
            w_out = sgu_w_out[o].astype(BF16)
            ws_p, bias_p = _sgu_spatial(sgu_w_s[o], sgu_b_s[o], s)
            ws_s, bias_s = _sgu_spatial(sgu_w_s[o], sgu_b_s[o], t)
            expand = lambda bias: jnp.repeat(bias.T, sgu_gdim, axis=1)
            gated, _ = _sgu(xp, norm_g[l, 1], w_in, sgu_norm_g[o], ws_p, expand(bias_p), False)
            xp = _ffn(xp, [(gated, w_out)], norm_g[l, 2], win_b, wdown_b, last)
            gated, zv = _sgu(xs, norm_g[l, 1], w_in, sgu_norm_g[o], ws_s, expand(bias_s), True)
            xs = _ffn(xs, [(gated, w_out)], norm_g[l, 2], win_b, wdown_b, last)
            zs_l.append(zv.reshape(bs, t, sgu_width))

    return (xp.reshape(b, s, d), xs.reshape(bs, t, d),
            jnp.stack(kp_l), jnp.stack(vp_l), jnp.stack(fp_l), jnp.stack(up_l),
            jnp.stack(ks_l), jnp.stack(vs_l), jnp.stack(fs_l), jnp.stack(us_l), jnp.stack(zs_l))
```

```python
import functools
import math

import jax
import jax.numpy as jnp
from jax import lax
from jax.experimental import pallas as pl
from jax.experimental.pallas import tpu as pltpu

F32 = jnp.float32
BF16 = jnp.bfloat16

EPS = 1e-6
FOX_HEAD_DIM = 64
POOL_WINDOWS = (2, 4, 8, 16)
POOL_HIST = max(POOL_WINDOWS) - 1
SGU_STREAM_CHUNK = 64
SGU_CHUNK = 128

V7X_LANES = 128
V7X_MXU_DIM = 256
V7X_VMEM_BYTES = 64 * 1024 * 1024
VMEM_COMPILER_RESERVE = 8 * 1024 * 1024

ROW_TILE = 1024
SUB_TILE = 512
FF_CHUNK = V7X_MXU_DIM
ATT_Q_BLOCK = 128
ATT_KV_CHUNK = V7X_MXU_DIM
AUG_LANES = V7X_LANES
N_SPLIT = 3
GATE_COPIES = 2 * N_SPLIT
LOG2E = math.log2(math.e)
GELU_A = math.sqrt(2.0 / math.pi)
GELU_B = 0.044715


def _vmem_limit(est_bytes):
    return int(min(max(est_bytes, 16 * 1024 * 1024), V7X_VMEM_BYTES - VMEM_COMPILER_RESERVE))


def _resident(shape):
    nd = len(shape)
    return pl.BlockSpec(shape, lambda *_: (0,) * nd, pipeline_mode=pl.Buffered(1))


def _rms(x, g):
    ms = jnp.mean(x * x, axis=-1, keepdims=True)
    return x * lax.rsqrt(ms + EPS) * g


def _round_up(n, m):
    return (n + m - 1) // m * m


def _ffn_body(n_mix, n_chunks, has_final, *refs):
    x_ref = refs[0]
    a_refs = refs[1:1 + n_mix]
    w_refs = refs[1 + n_mix:1 + 2 * n_mix]
    g_ref, win_ref, wdown_ref = refs[1 + 2 * n_mix:4 + 2 * n_mix]
    pos = 4 + 2 * n_mix
    fg_ref = refs[pos] if has_final else None
    pos += int(has_final)
    o_ref, h_scr, act_scr = refs[pos:pos + 3]

    d_ff = n_chunks * FF_CHUNK
    for r0 in range(0, x_ref.shape[0], SUB_TILE):
        rows = slice(r0, r0 + SUB_TILE)
        x = x_ref[rows, :]
        for a_ref, w_ref in zip(a_refs, w_refs):
            x = x + jnp.dot(a_ref[rows, :], w_ref[...], preferred_element_type=F32)
        o_ref[rows, :] = x
        h_scr[rows, :] = _rms(x, g_ref[...]).astype(BF16)
        for j in range(n_chunks):
            cols = slice(j * FF_CHUNK, (j + 1) * FF_CHUNK)
            up_cols = slice(d_ff + j * FF_CHUNK, d_ff + (j + 1) * FF_CHUNK)
            gate = jnp.dot(h_scr[rows, :], win_ref[:, cols], preferred_element_type=F32)
            up = jnp.dot(h_scr[rows, :], win_ref[:, up_cols], preferred_element_type=F32)
            act = gate * (1.0 / (1.0 + jnp.exp(-gate))) * up
            act_scr[rows, cols] = act.astype(BF16)
        y = jnp.dot(act_scr[rows, :], wdown_ref[...], preferred_element_type=F32)
        out = o_ref[rows, :] + 0.5 * y
        if has_final:
            out = _rms(out, fg_ref[...])
        o_ref[rows, :] = out


def _ffn(x, mixes, g, win3, wdown, final_g=None):
    rows, d = x.shape
    d_ff = wdown.shape[0]
    assert win3.shape == (d, 2 * d_ff) and d_ff % FF_CHUNK == 0
    n_chunks = d_ff // FF_CHUNK
    two_chunk = 2 * FF_CHUNK
    tm = ROW_TILE
    n_mix = len(mixes)
    has_final = final_g is not None

    row_spec = lambda width: pl.BlockSpec((tm, width), lambda i: (i, 0))
    in_specs = [row_spec(d)]
    in_specs += [row_spec(a.shape[1]) for a, _ in mixes]
    in_specs += [_resident(w.shape) for _, w in mixes]
    in_specs += [_resident((1, d)), _resident(win3.shape), _resident(wdown.shape)]
    args = [x] + [a for a, _ in mixes] + [w for _, w in mixes] + [g.reshape(1, d), win3, wdown]
    if has_final:
        in_specs.append(_resident((1, d)))
        args.append(final_g.reshape(1, d))

    est = (4 * tm * d * 4
           + sum(2 * tm * a.shape[1] * 2 + w.size * 2 for a, w in mixes)
           + (win3.size + wdown.size) * 2
           + tm * d * 2 + tm * d_ff * 2
           + SUB_TILE * two_chunk * 4 * 2 + SUB_TILE * d * 4 * 2)
    return pl.pallas_call(
        functools.partial(_ffn_body, n_mix, n_chunks, has_final),
        grid=(rows // tm,),
        in_specs=in_specs,
        out_specs=row_spec(d),
        out_shape=jax.ShapeDtypeStruct((rows, d), F32),
        scratch_shapes=[pltpu.VMEM((tm, d), BF16), pltpu.VMEM((tm, d_ff), BF16)],
        compiler_params=pltpu.CompilerParams(
            dimension_semantics=("arbitrary",), vmem_limit_bytes=_vmem_limit(est)),
        name="ffn_half_step",
    )(*args)


def _even_proj_body(width, heads, x_ref, g_ref, w_ref, bf_ref,
                    q_ref, kb_ref, vb_ref, k4_ref, v4_ref, u_ref, lf_ref, lfw_ref):
    for r0 in range(0, x_ref.shape[0], SUB_TILE):
        rows = slice(r0, r0 + SUB_TILE)
        h = _rms(x_ref[rows, :], g_ref[...]).astype(BF16)
        z = jnp.dot(h, w_ref[...], preferred_element_type=F32)
        q_ref[rows, :] = (z[:, :width] * (FOX_HEAD_DIM ** -0.5 * LOG2E)).astype(BF16)
        kb_ref[rows, :] = z[:, width:2 * width].astype(BF16)
        vb_ref[rows, :] = z[:, 2 * width:3 * width].astype(BF16)
        for hd in range(heads):
            head_rows = pl.ds(r0 * heads + hd, SUB_TILE, stride=heads)
            k4_ref[head_rows, :] = z[:, width + hd * FOX_HEAD_DIM:width + (hd + 1) * FOX_HEAD_DIM]
            v4_ref[head_rows, :] = z[:, 2 * width + hd * FOX_HEAD_DIM:2 * width + (hd + 1) * FOX_HEAD_DIM]
        u_ref[rows, :] = z[:, 3 * width:4 * width]
        f = z[:, 4 * width:4 * width + V7X_LANES] + bf_ref[...]
        logf = jnp.minimum(f, 0.0) - jnp.log(1.0 + jnp.exp(-jnp.abs(f)))
        lf_ref[rows, :] = logf[:, :heads]
        lfw_ref[rows, :] = logf


def _gate_lanes(heads):
    return [heads + GATE_COPIES * h + j for h in range(heads) for j in range(GATE_COPIES)], \
           [h for h in range(heads) for _ in range(GATE_COPIES)]


def _replicate_gates(cols, heads):
    lanes, src = _gate_lanes(heads)
    pad = V7X_LANES - heads - len(lanes)
    return jnp.concatenate([cols, jnp.take(cols, jnp.array(src), axis=-1),
                            jnp.zeros(cols.shape[:-1] + (pad,), cols.dtype)], axis=-1)


def _even_proj(x, g, w_all, bf_pad, width, heads):
    rows, d = x.shape
    tm = ROW_TILE
    n_out = w_all.shape[1]
    row_spec = lambda wd: pl.BlockSpec((tm, wd), lambda i: (i, 0))
    head_rows = pl.BlockSpec((tm * heads, FOX_HEAD_DIM), lambda i: (i, 0))
    est = (2 * tm * d * 4 + w_all.size * 2 + 2 * tm * width * (3 * 2 + 4)
           + 2 * 2 * tm * heads * V7X_LANES * 4
           + 4 * tm * V7X_LANES * 4 + tm * n_out * 4 * 2 + tm * d * 2)
    return pl.pallas_call(
        functools.partial(_even_proj_body, width, heads),
        grid=(rows // tm,),
        in_specs=[row_spec(d), _resident((1, d)), _resident(w_all.shape), _resident((1, V7X_LANES))],
        out_specs=[row_spec(width), row_spec(width), row_spec(width), head_rows, head_rows,
                   row_spec(width), row_spec(heads), row_spec(V7X_LANES)],
        out_shape=[jax.ShapeDtypeStruct((rows, width), BF16),
                   jax.ShapeDtypeStruct((rows, width), BF16),
                   jax.ShapeDtypeStruct((rows, width), BF16),
                   jax.ShapeDtypeStruct((rows * heads, FOX_HEAD_DIM), F32),
                   jax.ShapeDtypeStruct((rows * heads, FOX_HEAD_DIM), F32),
                   jax.ShapeDtypeStruct((rows, width), F32),
                   jax.ShapeDtypeStruct((rows, heads), F32),
                   jax.ShapeDtypeStruct((rows, V7X_LANES), F32)],
        compiler_params=pltpu.CompilerParams(
            dimension_semantics=("arbitrary",), vmem_limit_bytes=_vmem_limit(est)),
        name="even_projection",
    )(x, g.reshape(1, d), w_all, bf_pad)


def _attn_body(past, t_new, tq, heads, *refs):
    n_all = past + t_new
    n_pad = _round_up(n_all, V7X_LANES)
    if past:
        (q_ref, k_ref, v_ref, lf_ref, kp_ref, vp_ref, lfp_ref, u_ref, hist_ref, pw_ref, ps_ref,
         att_ref, pool_ref, ca, cb, c_hi, c_mid, c_lo, kcat, qcat, vcat, pa, pb) = refs
    else:
        (q_ref, k_ref, v_ref, lf_ref, u_ref, hist_ref, pw_ref, ps_ref,
         att_ref, pool_ref, ca, cb, c_hi, c_mid, c_lo, kcat, qcat, vcat, pa, pb) = refs
    pair = pl.program_id(1)

    @pl.when(pair == 0)
    def _():
        pad = 8
        ca[0:pad, :] = jnp.zeros((pad, V7X_LANES), F32)
        cb[0:pad, :] = jnp.zeros((pad, V7X_LANES), F32)
        if past:
            ca[pad:pad + past, :] = lfp_ref[...]
        ca[pad + past:pad + n_all, :] = lf_ref[...]
        src, dst = ca, cb
        s = 1
        while s < n_all:
            if s < pad:
                dst[pad:pad + n_all, :] = src[pad:pad + n_all, :] + src[pad - s:pad + n_all - s, :]
            else:
                dst[pad:pad + s, :] = src[pad:pad + s, :]
                dst[pad + s:pad + n_all, :] = src[pad + s:pad + n_all, :] + src[pad:pad + n_all - s, :]
            src, dst = dst, src
            s *= 2
        c = src[pad:pad + n_all, :] * LOG2E
        hi = c.astype(BF16)
        r1 = c - hi.astype(F32)
        mid = r1.astype(BF16)
        c_hi[...] = hi
        c_mid[...] = mid
        c_lo[...] = (r1 - mid.astype(F32)).astype(BF16)
        vcat[0:n_all, V7X_LANES:2 * V7X_LANES] = jnp.ones((n_all, V7X_LANES), BF16)
        if n_pad > n_all:
            kcat[n_all:n_pad, :] = jnp.zeros((n_pad - n_all, V7X_LANES + AUG_LANES), BF16)
            vcat[n_all:n_pad, :] = jnp.zeros((n_pad - n_all, 2 * V7X_LANES), BF16)

    ln = lax.broadcasted_iota(jnp.int32, (1, AUG_LANES), 1)
    base = heads + 2 * GATE_COPIES * pair

    def pat(lanes, value=1.0):
        out = jnp.zeros((1, AUG_LANES), F32)
        for l in lanes:
            out = jnp.where(ln == base + l, value, out)
        return out.astype(BF16)

    k_lanes = lambda j: [hh * GATE_COPIES + N_SPLIT + j for hh in range(2)]
    aug_k = (c_hi[...] * pat(k_lanes(0), -1.0) + c_mid[...] * pat(k_lanes(1), -1.0)
             + c_lo[...] * pat(k_lanes(2), -1.0)
             + pat([hh * GATE_COPIES + j for hh in range(2) for j in range(N_SPLIT)]))
    if past:
        def pair_slab(ref):
            return jnp.concatenate([ref[pl.ds(2 * pair + hh, past, stride=heads), :] for hh in range(2)], axis=1)
        kcat[0:past, 0:V7X_LANES] = pair_slab(kp_ref).astype(BF16)
        vcat[0:past, 0:V7X_LANES] = pair_slab(vp_ref).astype(BF16)
    kcat[past:n_all, 0:V7X_LANES] = k_ref[...]
    vcat[past:n_all, 0:V7X_LANES] = v_ref[...]
    kcat[0:n_all, V7X_LANES:V7X_LANES + AUG_LANES] = aug_k

    qv = q_ref[...]
    hi_q, mid_q, lo_q = c_hi[past:n_all, :], c_mid[past:n_all, :], c_lo[past:n_all, :]
    for hh in range(2):
        lo_lane = hh * GATE_COPIES
        aug_q = (hi_q * pat([lo_lane]) + mid_q * pat([lo_lane + 1]) + lo_q * pat([lo_lane + 2])
                 + pat(range(lo_lane + N_SPLIT, lo_lane + GATE_COPIES)))
        in_head = (ln >= hh * FOX_HEAD_DIM) & (ln < (hh + 1) * FOX_HEAD_DIM)
        q_h = qv * jnp.where(in_head, 1.0, 0.0).astype(BF16)
        for i in range(t_new // tq):
            qcat[i, hh * tq:(hh + 1) * tq, 0:V7X_LANES] = q_h[i * tq:(i + 1) * tq]
            qcat[i, hh * tq:(hh + 1) * tq, V7X_LANES:V7X_LANES + AUG_LANES] = aug_q[i * tq:(i + 1) * tq]

    nt = (((1,), (1,)), ((), ()))
    lane_o = lax.broadcasted_iota(jnp.int32, (tq, V7X_LANES), 1)
    for i in range(t_new // tq):
        r0 = i * tq
        kend = _round_up(past + r0 + tq, V7X_LANES)
        d0 = (past + r0) // V7X_LANES * V7X_LANES
        qs = qcat[i]
        bounds = [(c0, min(c0 + ATT_KV_CHUNK, kend)) for c0 in range(0, kend, ATT_KV_CHUNK)]
        s_list = []
        mx = None
        for c0, c1 in bounds:
            s = lax.dot_general(qs, kcat[c0:c1, :], nt, preferred_element_type=F32)
            if c1 > d0:
                qpos = jnp.bitwise_and(lax.broadcasted_iota(jnp.int32, (2 * tq, c1 - c0), 0), tq - 1) + (past + r0)
                kpos = lax.broadcasted_iota(jnp.int32, (2 * tq, c1 - c0), 1) + c0
                s = jnp.where(kpos <= qpos, s, -jnp.inf)
            s_list.append(s)
            fold = s[:, 0:V7X_LANES]
            for l0 in range(V7X_LANES, c1 - c0, V7X_LANES):
                fold = jnp.maximum(fold, s[:, l0:l0 + V7X_LANES])
            mx = fold if mx is None else jnp.maximum(mx, fold)
        m = jnp.max(mx, axis=1, keepdims=True)
        o = None
        for s, (c0, c1) in zip(s_list, bounds):
            pv = jnp.dot(jnp.exp2(s - m).astype(BF16), vcat[c0:c1, :], preferred_element_type=F32)
            o = pv if o is None else o + pv
        o = o[:, 0:V7X_LANES] / o[:, V7X_LANES:2 * V7X_LANES]
        att_ref[r0:r0 + tq, :] = jnp.where(lane_o < FOX_HEAD_DIM, o[0:tq], o[tq:2 * tq]).astype(BF16)

    lead = 2 * (POOL_HIST + 1)
    half = POOL_HIST + 1
    pa[0:half, :] = jnp.zeros((half, V7X_LANES), F32)
    pb[0:half, :] = jnp.zeros((half, V7X_LANES), F32)
    pa[half:lead, :] = hist_ref[...]
    pa[lead:lead + t_new, :] = u_ref[...]
    src, dst = pa, pb
    win = None
    shift = 1
    for gi in range(len(POOL_WINDOWS)):
        dst[half:lead + t_new, :] = src[half:lead + t_new, :] + src[half - shift:lead + t_new - shift, :]
        cur = dst[lead:lead + t_new, :]
        win = cur if win is None else jnp.where(pair >= gi, cur, win)
        src, dst = dst, src
        shift *= 2
    width = jnp.left_shift(2, pair)
    posn = lax.broadcasted_iota(jnp.int32, (t_new, V7X_LANES), 0) + past
    cnt = jnp.minimum(width, posn + 1).astype(F32)
    dlt = win / cnt - u_ref[...]
    y = jnp.dot(dlt.astype(BF16), pw_ref[...], preferred_element_type=F32)
    pool_ref[...] = (y * ps_ref[...]).astype(BF16)


def _attention_pool(heads, q, k, v, logf_wide, u, hist16, pool_w, pool_scale, past_kvf=None):
    b, t_new, width = q.shape
    n_pairs = width // V7X_LANES
    assert heads + heads * GATE_COPIES <= AUG_LANES
    past = 0 if past_kvf is None else past_kvf[2].shape[1]
    n_all = past + t_new
    n_pad = _round_up(n_all, V7X_LANES)
    tq = min(ATT_Q_BLOCK, t_new)
    assert tq & (tq - 1) == 0 and t_new % tq == 0 and heads % 2 == 0

    slab = lambda rows: pl.BlockSpec((None, rows, V7X_LANES), lambda bi, pi: (bi, 0, pi))
    gates = lambda rows: pl.BlockSpec((None, rows, V7X_LANES), lambda bi, pi: (bi, 0, 0))
    in_specs = [slab(t_new), slab(t_new), slab(t_new), gates(t_new)]
    args = [q, k, v, logf_wide]
    if past:
        head_rows = pl.BlockSpec((None, past * heads, FOX_HEAD_DIM), lambda bi, pi: (bi, 0, 0))
        in_specs += [head_rows, head_rows, gates(past)]
        args += list(past_kvf)
    in_specs += [slab(t_new), slab(POOL_HIST + 1),
                 pl.BlockSpec((None, V7X_LANES, V7X_LANES), lambda bi, pi: (pi, 0, 0)),
                 pl.BlockSpec((None, 1, V7X_LANES), lambda bi, pi: (pi, 0, 0))]
    args += [u, hist16, pool_w, pool_scale]

    lane_pad_f32 = V7X_LANES * 4
    scratch = [
        pltpu.VMEM((8 + n_all, V7X_LANES), F32), pltpu.VMEM((8 + n_all, V7X_LANES), F32),
        pltpu.VMEM((n_all, AUG_LANES), BF16), pltpu.VMEM((n_all, AUG_LANES), BF16),
        pltpu.VMEM((n_all, AUG_LANES), BF16),
        pltpu.VMEM((n_pad, V7X_LANES + AUG_LANES), BF16),
        pltpu.VMEM((t_new // tq, 2 * tq, V7X_LANES + AUG_LANES), BF16),
        pltpu.VMEM((n_pad, 2 * V7X_LANES), BF16),
        pltpu.VMEM((2 * (POOL_HIST + 1) + t_new, V7X_LANES), F32),
        pltpu.VMEM((2 * (POOL_HIST + 1) + t_new, V7X_LANES), F32),
    ]
    est = (2 * t_new * V7X_LANES * (2 + 2 + 2 + 4 + 2 + 2) + 2 * t_new * lane_pad_f32
           + 2 * past * (2 * heads * lane_pad_f32 + lane_pad_f32)
           + 3 * (8 + n_all) * lane_pad_f32 + n_all * lane_pad_f32
           + 2 * n_pad * (V7X_LANES + AUG_LANES) * 2 + 2 * t_new * (V7X_LANES + AUG_LANES) * 2
           + 2 * (32 + t_new) * lane_pad_f32
           + 4 * 2 * tq * n_pad * 4 + 10 * n_all * lane_pad_f32)
    return pl.pallas_call(
        functools.partial(_attn_body, past, t_new, tq, heads),
        grid=(b, n_pairs),
        in_specs=in_specs,
        out_specs=[slab(t_new), slab(t_new)],
        out_shape=[jax.ShapeDtypeStruct((b, t_new, width), BF16),
                   jax.ShapeDtypeStruct((b, t_new, width), BF16)],
        scratch_shapes=scratch,
        compiler_params=pltpu.CompilerParams(
            dimension_semantics=("arbitrary", "arbitrary"), vmem_limit_bytes=_vmem_limit(est)),
        name="fox_attention_pool",
    )(*args)


def _sgu_body(groups, want_zv, x_ref, g_ref, w_ref, ng_ref, ws_ref, bias_ref, *rest):
    if want_zv:
        o_ref, zv_ref, zvb_scr = rest
    else:
        o_ref, zvb_scr = rest
        zv_ref = None
    width = ng_ref.shape[1]
    gdim = width // groups
    for r0 in range(0, x_ref.shape[0], SUB_TILE):
        rows = slice(r0, r0 + SUB_TILE)
        h = _rms(x_ref[rows, :], g_ref[...]).astype(BF16)
        z = jnp.dot(h, w_ref[...], preferred_element_type=F32)
        half_z = 0.5 * z
        z = half_z + half_z * jnp.tanh(z * (GELU_A + (GELU_A * GELU_B) * (z * z)))
        zv = _rms(z[:, width:], ng_ref[...])
        if want_zv:
            zv_ref[rows, :] = zv
        zvb_scr[rows, :] = zv.astype(BF16)
        for ch in range(SUB_TILE // SGU_CHUNK):
            rs = slice(ch * SGU_CHUNK, (ch + 1) * SGU_CHUNK)
            out_rs = slice(r0 + ch * SGU_CHUNK, r0 + (ch + 1) * SGU_CHUNK)
            for gi in range(groups):
                cs = slice(gi * gdim, (gi + 1) * gdim)
                mix = jnp.dot(ws_ref[gi], zvb_scr[out_rs, cs], preferred_element_type=F32) + bias_ref[:, cs]
                o_ref[out_rs, cs] = (z[rs, cs] * mix).astype(BF16)


def _sgu(x, g, w_in, norm_g, ws_mat, bias_full, want_zv):
    rows, d = x.shape
    width = norm_g.shape[0]
    groups = ws_mat.shape[0]
    tm = ROW_TILE
    row_spec = lambda wd: pl.BlockSpec((tm, wd), lambda i: (i, 0))
    out_specs = [row_spec(width)]
    out_shape = [jax.ShapeDtypeStruct((rows, width), BF16)]
    if want_zv:
        out_specs.append(row_spec(width))
        out_shape.append(jax.ShapeDtypeStruct((rows, width), F32))
    est = (2 * tm * d * 4 + w_in.size * 2 + ws_mat.size * 2 + bias_full.size * 4
           + 2 * tm * width * 2 + (2 * tm * width * 4 if want_zv else 0)
           + tm * width * 2 + 3 * tm * 2 * width * 4)
    res = pl.pallas_call(
        functools.partial(_sgu_body, groups, want_zv),
        grid=(rows // tm,),
        in_specs=[row_spec(d), _resident((1, d)), _resident(w_in.shape), _resident((1, width)),
                  _resident(ws_mat.shape), _resident(bias_full.shape)],
        out_specs=out_specs,
        out_shape=out_shape,
        scratch_shapes=[pltpu.VMEM((tm, width), BF16)],
        compiler_params=pltpu.CompilerParams(
            dimension_semantics=("arbitrary",), vmem_limit_bytes=_vmem_limit(est)),
        name="spatial_gating",
    )(x, g.reshape(1, d), w_in, norm_g.reshape(1, width), ws_mat, bias_full)
    return res if want_zv else (res[0], None)


def _prep_ffn(w_in, w_down):
    return w_in.astype(BF16), w_down.astype(BF16)


def _sgu_spatial(w_s, b_s, seq_len):
    groups = w_s.shape[0]
    span = min(seq_len, SGU_CHUNK)
    idx = jnp.arange(span) // SGU_STREAM_CHUNK
    w = jnp.where((idx[None, :] <= idx[:, None])[None], w_s[:, :span, :span], 0.0)
    reps = SGU_CHUNK // span
    if reps > 1:
        w = jnp.einsum('ab,gts->gatbs', jnp.eye(reps, dtype=w.dtype), w).reshape(groups, SGU_CHUNK, SGU_CHUNK)
    bias = jnp.tile(b_s[:, :span], (1, reps))
    return w.astype(BF16), bias


def kernel(x_prompt, x_sample, cache_k, cache_v, cache_logf, state_pool, norm_g, ffn_w_in, ffn_w_down,
           even_w_in, even_b_f, pool_w, pool_scale, even_w_out, sgu_w_in, sgu_norm_g, sgu_w_s, sgu_b_s,
           sgu_w_out, final_g):
    b, s, d = x_prompt.shape
    bs, t, _ = x_sample.shape
    depth = norm_g.shape[0]
    heads = even_b_f.shape[1]
    width = heads * FOX_HEAD_DIM
    pool_width = pool_scale.shape[1]
    n_groups = pool_w.shape[1]
    past = cache_k.shape[2]
    sgu_width = sgu_norm_g.shape[1]
    sgu_groups = sgu_w_s.shape[1]
    sgu_gdim = sgu_width // sgu_groups

    xp = x_prompt.reshape(b * s, d)
    xs = x_sample.reshape(bs * t, d)
    kp_l, vp_l, fp_l, up_l = [], [], [], []
    ks_l, vs_l, fs_l, us_l, zs_l = [], [], [], [], []

    for l in range(depth):
        win_a, wdown_a = _prep_ffn(ffn_w_in[l, 0], ffn_w_down[l, 0])
        win_b, wdown_b = _prep_ffn(ffn_w_in[l, 1], ffn_w_down[l, 1])
        last = final_g if l == depth - 1 else None
        xp = _ffn(xp, [], norm_g[l, 0], win_a, wdown_a)
        xs = _ffn(xs, [], norm_g[l, 0], win_a, wdown_a)
        if l % 2 == 0:
            e = l // 2
            w_e = even_w_in[e]
            f_cols = _replicate_gates(w_e[:, 3 * width:3 * width + heads], heads)
            w_all = jnp.concatenate([w_e[:, :3 * width], w_e[:, 3 * width + heads:], f_cols], axis=1).astype(BF16)
            bf_pad = _replicate_gates(even_b_f[e], heads).reshape(1, V7X_LANES)
            pw = pool_w[e].astype(BF16)
            psc = pool_scale[e].reshape(n_groups, 1, pool_width // n_groups)
            w_out_att = even_w_out[e, :width].astype(BF16)
            w_out_pool = even_w_out[e, width:].astype(BF16)

            q, kb, vb, k, v, u, lf, lfw = _even_proj(xp, norm_g[l, 1], w_all, bf_pad, width, heads)
            hist = jnp.zeros((b, POOL_HIST + 1, pool_width), F32)
            att, pool = _attention_pool(heads, q.reshape(b, s, width), kb.reshape(b, s, width),
                                        vb.reshape(b, s, width), lfw.reshape(b, s, V7X_LANES),
                                        u.reshape(b, s, pool_width), hist, pw, psc)
            xp = _ffn(xp, [(att.reshape(b * s, width), w_out_att), (pool.reshape(b * s, pool_width), w_out_pool)],
                      norm_g[l, 2], win_b, wdown_b, last)
            kp_l.append(k.reshape(b, s, heads, FOX_HEAD_DIM))
            vp_l.append(v.reshape(b, s, heads, FOX_HEAD_DIM))
            fp_l.append(lf.reshape(b, s, heads))
            up_l.append(u.reshape(b, s, pool_width)[:, s - POOL_HIST:])

            q, kb, vb, k, v, u, lf, lfw = _even_proj(xs, norm_g[l, 1], w_all, bf_pad, width, heads)
            hist = jnp.pad(state_pool[e], ((0, 0), (1, 0), (0, 0)))
            past_kvf = (cache_k[e].reshape(bs, past * heads, FOX_HEAD_DIM),
                        cache_v[e].reshape(bs, past * heads, FOX_HEAD_DIM),
                        _replicate_gates(cache_logf[e], heads))
            att, pool = _attention_pool(heads, q.reshape(bs, t, width), kb.reshape(bs, t, width),
                                        vb.reshape(bs, t, width), lfw.reshape(bs, t, V7X_LANES),
                                        u.reshape(bs, t, pool_width), hist, pw, psc, past_kvf)
            xs = _ffn(xs, [(att.reshape(bs * t, width), w_out_att), (pool.reshape(bs * t, pool_width), w_out_pool)],
                      norm_g[l, 2], win_b, wdown_b, last)
            ks_l.append(k.reshape(bs, t, heads, FOX_HEAD_DIM))
            vs_l.append(v.reshape(bs, t, heads, FOX_HEAD_DIM))
            fs_l.append(lf.reshape(bs, t, heads))
            u_ext = jnp.concatenate([state_pool[e], u.reshape(bs, t, pool_width)], axis=1)
            us_l.append(u_ext[:, -POOL_HIST:])
        else:
            o = l // 2
            w_in = sgu_w_in[o].astype(BF16)
            w_out = sgu_w_out[o].astype(BF16)
            ws_p, bias_p = _sgu_spatial(sgu_w_s[o], sgu_b_s[o], s)
            ws_s, bias_s = _sgu_spatial(sgu_w_s[o], sgu_b_s[o], t)
            expand = lambda bias: jnp.repeat(bias.T, sgu_gdim, axis=1)
            gated, _ = _sgu(xp, norm_g[l, 1], w_in, sgu_norm_g[o], ws_p, expand(bias_p), False)
            xp = _ffn(xp, [(gated, w_out)], norm_g[l, 2], win_b, wdown_b, last)
            gated, zv = _sgu(xs, norm_g[l, 1], w_in, sgu_norm_g[o], ws_s, expand(bias_s), True)
            xs = _ffn(xs, [(gated, w_out)], norm_g[l, 2], win_b, wdown_b, last)
            zs_l.append(zv.reshape(bs, t, sgu_width))

    return (xp.reshape(b, s, d), xs.reshape(bs, t, d),
            jnp.stack(kp_l), jnp.stack(vp_l), jnp.stack(fp_l), jnp.stack(up_l),
            jnp.stack(ks_l), jnp.stack(vs_l), jnp.stack(fs_l), jnp.stack(us_l), jnp.stack(zs_l))
```

```python
import functools
import math

import jax
import jax.numpy as jnp
from jax import lax
from jax.experimental import pallas as pl
from jax.experimental.pallas import tpu as pltpu

F32 = jnp.float32
BF16 = jnp.bfloat16

EPS = 1e-6
FOX_HEAD_DIM = 64
POOL_WINDOWS = (2, 4, 8, 16)
POOL_HIST = max(POOL_WINDOWS) - 1
SGU_STREAM_CHUNK = 64
SGU_CHUNK = 128

V7X_LANES = 128
V7X_MXU_DIM = 256
V7X_VMEM_BYTES = 64 * 1024 * 1024
VMEM_COMPILER_RESERVE = 8 * 1024 * 1024

ROW_TILE = 1024
SUB_TILE = 512
FF_CHUNK = V7X_MXU_DIM
ATT_Q_BLOCK = 128
ATT_KV_CHUNK = V7X_MXU_DIM
AUG_LANES = V7X_LANES
N_SPLIT = 3
GATE_COPIES = 2 * N_SPLIT
LOG2E = math.log2(math.e)
GELU_A = math.sqrt(2.0 / math.pi)
GELU_B = 0.044715


def _vmem_limit(est_bytes):
    return int(min(max(est_bytes, 16 * 1024 * 1024), V7X_VMEM_BYTES - VMEM_COMPILER_RESERVE))


def _resident(shape):
    nd = len(shape)
    return pl.BlockSpec(shape, lambda *_: (0,) * nd, pipeline_mode=pl.Buffered(1))


def _rms(x, g):
    ms = jnp.mean(x * x, axis=-1, keepdims=True)
    return x * lax.rsqrt(ms + EPS) * g


def _round_up(n, m):
    return (n + m - 1) // m * m


def _ffn_body(n_mix, n_chunks, has_final, *refs):
    x_ref = refs[0]
    a_refs = refs[1:1 + n_mix]
    w_refs = refs[1 + n_mix:1 + 2 * n_mix]
    g_ref, win_ref, wdown_ref = refs[1 + 2 * n_mix:4 + 2 * n_mix]
    pos = 4 + 2 * n_mix
    fg_ref = refs[pos] if has_final else None
    pos += int(has_final)
    o_ref, h_scr, act_scr = refs[pos:pos + 3]

    d_ff = n_chunks * FF_CHUNK
    for r0 in range(0, x_ref.shape[0], SUB_TILE):
        rows = slice(r0, r0 + SUB_TILE)
        x = x_ref[rows, :]
        for a_ref, w_ref in zip(a_refs, w_refs):
            x = x + jnp.dot(a_ref[rows, :], w_ref[...], preferred_element_type=F32)
        o_ref[rows, :] = x
        h_scr[rows, :] = _rms(x, g_ref[...]).astype(BF16)
        for j in range(n_chunks):
            cols = slice(j * FF_CHUNK, (j + 1) * FF_CHUNK)
            up_cols = slice(d_ff + j * FF_CHUNK, d_ff + (j + 1) * FF_CHUNK)
            gate = jnp.dot(h_scr[rows, :], win_ref[:, cols], preferred_element_type=F32)
            up = jnp.dot(h_scr[rows, :], win_ref[:, up_cols], preferred_element_type=F32)
            act = gate * (1.0 / (1.0 + jnp.exp(-gate))) * up
            act_scr[rows, cols] = act.astype(BF16)
        y = jnp.dot(act_scr[rows, :], wdown_ref[...], preferred_element_type=F32)
        out = o_ref[rows, :] + 0.5 * y
        if has_final:
            out = _rms(out, fg_ref[...])
        o_ref[rows, :] = out


def _ffn(x, mixes, g, win3, wdown, final_g=None):
    rows, d = x.shape
    d_ff = wdown.shape[0]
    assert win3.shape == (d, 2 * d_ff) and d_ff % FF_CHUNK == 0
    n_chunks = d_ff // FF_CHUNK
    two_chunk = 2 * FF_CHUNK
    tm = ROW_TILE
    n_mix = len(mixes)
    has_final = final_g is not None

    row_spec = lambda width: pl.BlockSpec((tm, width), lambda i: (i, 0))
    in_specs = [row_spec(d)]
    in_specs += [row_spec(a.shape[1]) for a, _ in mixes]
    in_specs += [_resident(w.shape) for _, w in mixes]
    in_specs += [_resident((1, d)), _resident(win3.shape), _resident(wdown.shape)]
    args = [x] + [a for a, _ in mixes] + [w for _, w in mixes] + [g.reshape(1, d), win3, wdown]
    if has_final:
        in_specs.append(_resident((1, d)))
        args.append(final_g.reshape(1, d))

    est = (4 * tm * d * 4
           + sum(2 * tm * a.shape[1] * 2 + w.size * 2 for a, w in mixes)
           + (win3.size + wdown.size) * 2
           + tm * d * 2 + tm * d_ff * 2
           + SUB_TILE * two_chunk * 4 * 2 + SUB_TILE * d * 4 * 2)
    return pl.pallas_call(
        functools.partial(_ffn_body, n_mix, n_chunks, has_final),
        grid=(rows // tm,),
        in_specs=in_specs,
        out_specs=row_spec(d),
        out_shape=jax.ShapeDtypeStruct((rows, d), F32),
        scratch_shapes=[pltpu.VMEM((tm, d), BF16), pltpu.VMEM((tm, d_ff), BF16)],
        compiler_params=pltpu.CompilerParams(
            dimension_semantics=("arbitrary",), vmem_limit_bytes=_vmem_limit(est)),
        name="ffn_half_step",
    )(*args)


def _even_proj_body(width, heads, x_ref, g_ref, w_ref, bf_ref,
                    q_ref, kb_ref, vb_ref, k4_ref, v4_ref, u_ref, lf_ref, lfw_ref):
    starts = range(0, x_ref.shape[0], SUB_TILE)
    zs = [jnp.dot(_rms(x_ref[r0:r0 + SUB_TILE, :], g_ref[...]).astype(BF16), w_ref[...],
                  preferred_element_type=F32) for r0 in starts]
    for r0, z in zip(starts, zs):
        rows = slice(r0, r0 + SUB_TILE)
        q_ref[rows, :] = (z[:, :width] * (FOX_HEAD_DIM ** -0.5 * LOG2E)).astype(BF16)
        kb_ref[rows, :] = z[:, width:2 * width].astype(BF16)
        vb_ref[rows, :] = z[:, 2 * width:3 * width].astype(BF16)
        for hd in range(heads):
            head_rows = pl.ds(r0 * heads + hd, SUB_TILE, stride=heads)
            k4_ref[head_rows, :] = z[:, width + hd * FOX_HEAD_DIM:width + (hd + 1) * FOX_HEAD_DIM]
            v4_ref[head_rows, :] = z[:, 2 * width + hd * FOX_HEAD_DIM:2 * width + (hd + 1) * FOX_HEAD_DIM]
        u_ref[rows, :] = z[:, 3 * width:4 * width]
        f = z[:, 4 * width:4 * width + V7X_LANES] + bf_ref[...]
        logf = jnp.minimum(f, 0.0) - jnp.log(1.0 + jnp.exp(-jnp.abs(f)))
        lf_ref[rows, :] = logf[:, :heads]
        lfw_ref[rows, :] = logf


def _gate_lanes(heads):
    return [heads + GATE_COPIES * h + j for h in range(heads) for j in range(GATE_COPIES)], \
           [h for h in range(heads) for _ in range(GATE_COPIES)]


def _replicate_gates(cols, heads):
    lanes, src = _gate_lanes(heads)
    pad = V7X_LANES - heads - len(lanes)
    return jnp.concatenate([cols, jnp.take(cols, jnp.array(src), axis=-1),
                            jnp.zeros(cols.shape[:-1] + (pad,), cols.dtype)], axis=-1)


def _even_proj(x, g, w_all, bf_pad, width, heads):
    rows, d = x.shape
    tm = ROW_TILE
    n_out = w_all.shape[1]
    row_spec = lambda wd: pl.BlockSpec((tm, wd), lambda i: (i, 0))
    head_rows = pl.BlockSpec((tm * heads, FOX_HEAD_DIM), lambda i: (i, 0))
    est = (2 * tm * d * 4 + w_all.size * 2 + 2 * tm * width * (3 * 2 + 4)
           + 2 * 2 * tm * heads * V7X_LANES * 4
           + 4 * tm * V7X_LANES * 4 + tm * n_out * 4 * 2 + tm * d * 2)
    return pl.pallas_call(
        functools.partial(_even_proj_body, width, heads),
        grid=(rows // tm,),
        in_specs=[row_spec(d), _resident((1, d)), _resident(w_all.shape), _resident((1, V7X_LANES))],
        out_specs=[row_spec(width), row_spec(width), row_spec(width), head_rows, head_rows,
                   row_spec(width), row_spec(heads), row_spec(V7X_LANES)],
        out_shape=[jax.ShapeDtypeStruct((rows, width), BF16),
                   jax.ShapeDtypeStruct((rows, width), BF16),
                   jax.ShapeDtypeStruct((rows, width), BF16),
                   jax.ShapeDtypeStruct((rows * heads, FOX_HEAD_DIM), F32),
                   jax.ShapeDtypeStruct((rows * heads, FOX_HEAD_DIM), F32),
                   jax.ShapeDtypeStruct((rows, width), F32),
                   jax.ShapeDtypeStruct((rows, heads), F32),
                   jax.ShapeDtypeStruct((rows, V7X_LANES), F32)],
        compiler_params=pltpu.CompilerParams(
            dimension_semantics=("arbitrary",), vmem_limit_bytes=_vmem_limit(est)),
        name="even_projection",
    )(x, g.reshape(1, d), w_all, bf_pad)


def _attn_body(past, t_new, tq, heads, *refs):
    n_all = past + t_new
    n_pad = _round_up(n_all, V7X_LANES)
    if past:
        (q_ref, k_ref, v_ref, lf_ref, kp_ref, vp_ref, lfp_ref, u_ref, hist_ref, pw_ref, ps_ref,
         att_ref, pool_ref, ca, cb, c_hi, c_mid, c_lo, kcat, qcat, vcat, pa, pb) = refs
    else:
        (q_ref, k_ref, v_ref, lf_ref, u_ref, hist_ref, pw_ref, ps_ref,
         att_ref, pool_ref, ca, cb, c_hi, c_mid, c_lo, kcat, qcat, vcat, pa, pb) = refs
    pair = pl.program_id(1)

    @pl.when(pair == 0)
    def _():
        pad = 8
        ca[0:pad, :] = jnp.zeros((pad, V7X_LANES), F32)
        cb[0:pad, :] = jnp.zeros((pad, V7X_LANES), F32)
        if past:
            ca[pad:pad + past, :] = lfp_ref[...]
        ca[pad + past:pad + n_all, :] = lf_ref[...]
        src, dst = ca, cb
        s = 1
        while s < n_all:
            if s < pad:
                dst[pad:pad + n_all, :] = src[pad:pad + n_all, :] + src[pad - s:pad + n_all - s, :]
            else:
                dst[pad:pad + s, :] = src[pad:pad + s, :]
                dst[pad + s:pad + n_all, :] = src[pad + s:pad + n_all, :] + src[pad:pad + n_all - s, :]
            src, dst = dst, src
            s *= 2
        c = src[pad:pad + n_all, :] * LOG2E
        hi = c.astype(BF16)
        r1 = c - hi.astype(F32)
        mid = r1.astype(BF16)
        c_hi[...] = hi
        c_mid[...] = mid
        c_lo[...] = (r1 - mid.astype(F32)).astype(BF16)
        vcat[0:n_all, V7X_LANES:2 * V7X_LANES] = jnp.ones((n_all, V7X_LANES), BF16)
        if n_pad > n_all:
            kcat[n_all:n_pad, :] = jnp.zeros((n_pad - n_all, V7X_LANES + AUG_LANES), BF16)
            vcat[n_all:n_pad, :] = jnp.zeros((n_pad - n_all, 2 * V7X_LANES), BF16)

    ln = lax.broadcasted_iota(jnp.int32, (1, AUG_LANES), 1)
    base = heads + 2 * GATE_COPIES * pair

    def pat(lanes, value=1.0):
        out = jnp.zeros((1, AUG_LANES), F32)
        for l in lanes:
            out = jnp.where(ln == base + l, value, out)
        return out.astype(BF16)

    k_lanes = lambda j: [hh * GATE_COPIES + N_SPLIT + j for hh in range(2)]
    aug_k = (c_hi[...] * pat(k_lanes(0), -1.0) + c_mid[...] * pat(k_lanes(1), -1.0)
             + c_lo[...] * pat(k_lanes(2), -1.0)
             + pat([hh * GATE_COPIES + j for hh in range(2) for j in range(N_SPLIT)]))
    if past:
        kcat[0:past, 0:V7X_LANES] = kp_ref[...].T.astype(BF16)
        vcat[0:past, 0:V7X_LANES] = vp_ref[...].T.astype(BF16)
    kcat[past:n_all, 0:V7X_LANES] = k_ref[...]
    vcat[past:n_all, 0:V7X_LANES] = v_ref[...]
    kcat[0:n_all, V7X_LANES:V7X_LANES + AUG_LANES] = aug_k

    qv = q_ref[...]
    hi_q, mid_q, lo_q = c_hi[past:n_all, :], c_mid[past:n_all, :], c_lo[past:n_all, :]
    for hh in range(2):
        lo_lane = hh * GATE_COPIES
        aug_q = (hi_q * pat([lo_lane]) + mid_q * pat([lo_lane + 1]) + lo_q * pat([lo_lane + 2])
                 + pat(range(lo_lane + N_SPLIT, lo_lane + GATE_COPIES)))
        in_head = (ln >= hh * FOX_HEAD_DIM) & (ln < (hh + 1) * FOX_HEAD_DIM)
        q_h = qv * jnp.where(in_head, 1.0, 0.0).astype(BF16)
        for i in range(t_new // tq):
            qcat[i, hh * tq:(hh + 1) * tq, 0:V7X_LANES] = q_h[i * tq:(i + 1) * tq]
            qcat[i, hh * tq:(hh + 1) * tq, V7X_LANES:V7X_LANES + AUG_LANES] = aug_q[i * tq:(i + 1) * tq]

    nt = (((1,), (1,)), ((), ()))
    lane_o = lax.broadcasted_iota(jnp.int32, (tq, V7X_LANES), 1)
    for i in range(t_new // tq):
        r0 = i * tq
        kend = _round_up(past + r0 + tq, V7X_LANES)
        d0 = (past + r0) // V7X_LANES * V7X_LANES
        qs = qcat[i]
        bounds = [(c0, min(c0 + ATT_KV_CHUNK, kend)) for c0 in range(0, kend, ATT_KV_CHUNK)]
        s_list = []
        mx = None
        for c0, c1 in bounds:
            s = lax.dot_general(qs, kcat[c0:c1, :], nt, preferred_element_type=F32)
            if c1 > d0:
                qpos = jnp.bitwise_and(lax.broadcasted_iota(jnp.int32, (2 * tq, c1 - c0), 0), tq - 1) + (past + r0)
                kpos = lax.broadcasted_iota(jnp.int32, (2 * tq, c1 - c0), 1) + c0
                s = jnp.where(kpos <= qpos, s, -jnp.inf)
            s_list.append(s)
            fold = s[:, 0:V7X_LANES]
            for l0 in range(V7X_LANES, c1 - c0, V7X_LANES):
                fold = jnp.maximum(fold, s[:, l0:l0 + V7X_LANES])
            mx = fold if mx is None else jnp.maximum(mx, fold)
        m = jnp.max(mx, axis=1, keepdims=True)
        o = None
        for s, (c0, c1) in zip(s_list, bounds):
            pv = jnp.dot(jnp.exp2(s - m).astype(BF16), vcat[c0:c1, :], preferred_element_type=F32)
            o = pv if o is None else o + pv
        o = o[:, 0:V7X_LANES] / o[:, V7X_LANES:2 * V7X_LANES]
        att_ref[r0:r0 + tq, :] = jnp.where(lane_o < FOX_HEAD_DIM, o[0:tq], o[tq:2 * tq]).astype(BF16)

    lead = 2 * (POOL_HIST + 1)
    half = POOL_HIST + 1
    pa[0:half, :] = jnp.zeros((half, V7X_LANES), F32)
    pb[0:half, :] = jnp.zeros((half, V7X_LANES), F32)
    pa[half:lead, :] = hist_ref[...]
    pa[lead:lead + t_new, :] = u_ref[...]
    src, dst = pa, pb
    win = None
    shift = 1
    for gi in range(len(POOL_WINDOWS)):
        dst[half:lead + t_new, :] = src[half:lead + t_new, :] + src[half - shift:lead + t_new - shift, :]
        cur = dst[lead:lead + t_new, :]
        win = cur if win is None else jnp.where(pair >= gi, cur, win)
        src, dst = dst, src
        shift *= 2
    width = jnp.left_shift(2, pair)
    posn = lax.broadcasted_iota(jnp.int32, (t_new, V7X_LANES), 0) + past
    cnt = jnp.minimum(width, posn + 1).astype(F32)
    dlt = win / cnt - u_ref[...]
    y = jnp.dot(dlt.astype(BF16), pw_ref[...], preferred_element_type=F32)
    pool_ref[...] = (y * ps_ref[...]).astype(BF16)


def _attention_pool(heads, q, k, v, logf_wide, u, hist16, pool_w, pool_scale, past_kvf=None):
    b, t_new, width = q.shape
    n_pairs = width // V7X_LANES
    assert heads + heads * GATE_COPIES <= AUG_LANES
    past = 0 if past_kvf is None else past_kvf[2].shape[1]
    n_all = past + t_new
    n_pad = _round_up(n_all, V7X_LANES)
    tq = min(ATT_Q_BLOCK, t_new)
    assert tq & (tq - 1) == 0 and t_new % tq == 0 and heads % 2 == 0

    slab = lambda rows: pl.BlockSpec((None, rows, V7X_LANES), lambda bi, pi: (bi, 0, pi))
    gates = lambda rows: pl.BlockSpec((None, rows, V7X_LANES), lambda bi, pi: (bi, 0, 0))
    in_specs = [slab(t_new), slab(t_new), slab(t_new), gates(t_new)]
    args = [q, k, v, logf_wide]
    if past:
        channel_major = pl.BlockSpec((None, V7X_LANES, past), lambda bi, pi: (bi, pi, 0))
        in_specs += [channel_major, channel_major, gates(past)]
        args += list(past_kvf)
    in_specs += [slab(t_new), slab(POOL_HIST + 1),
                 pl.BlockSpec((None, V7X_LANES, V7X_LANES), lambda bi, pi: (pi, 0, 0)),
                 pl.BlockSpec((None, 1, V7X_LANES), lambda bi, pi: (pi, 0, 0))]
    args += [u, hist16, pool_w, pool_scale]

    lane_pad_f32 = V7X_LANES * 4
    scratch = [
        pltpu.VMEM((8 + n_all, V7X_LANES), F32), pltpu.VMEM((8 + n_all, V7X_LANES), F32),
        pltpu.VMEM((n_all, AUG_LANES), BF16), pltpu.VMEM((n_all, AUG_LANES), BF16),
        pltpu.VMEM((n_all, AUG_LANES), BF16),
        pltpu.VMEM((n_pad, V7X_LANES + AUG_LANES), BF16),
        pltpu.VMEM((t_new // tq, 2 * tq, V7X_LANES + AUG_LANES), BF16),
        pltpu.VMEM((n_pad, 2 * V7X_LANES), BF16),
        pltpu.VMEM((2 * (POOL_HIST + 1) + t_new, V7X_LANES), F32),
        pltpu.VMEM((2 * (POOL_HIST + 1) + t_new, V7X_LANES), F32),
    ]
    est = (2 * t_new * V7X_LANES * (2 + 2 + 2 + 4 + 2 + 2) + 2 * t_new * lane_pad_f32
           + 2 * past * (2 * V7X_LANES * 4 + lane_pad_f32)
           + 3 * (8 + n_all) * lane_pad_f32 + n_all * lane_pad_f32
           + 2 * n_pad * (V7X_LANES + AUG_LANES) * 2 + 2 * t_new * (V7X_LANES + AUG_LANES) * 2
           + 2 * (32 + t_new) * lane_pad_f32
           + 4 * 2 * tq * n_pad * 4 + 10 * n_all * lane_pad_f32)
    return pl.pallas_call(
        functools.partial(_attn_body, past, t_new, tq, heads),
        grid=(b, n_pairs),
        in_specs=in_specs,
        out_specs=[slab(t_new), slab(t_new)],
        out_shape=[jax.ShapeDtypeStruct((b, t_new, width), BF16),
                   jax.ShapeDtypeStruct((b, t_new, width), BF16)],
        scratch_shapes=scratch,
        compiler_params=pltpu.CompilerParams(
            dimension_semantics=("arbitrary", "arbitrary"), vmem_limit_bytes=_vmem_limit(est)),
        name="fox_attention_pool",
    )(*args)


def _sgu_body(groups, want_zv, x_ref, g_ref, w_ref, ng_ref, ws_ref, bias_ref, *rest):
    if want_zv:
        o_ref, zv_ref, zvb_scr = rest
    else:
        o_ref, zvb_scr = rest
        zv_ref = None
    width = ng_ref.shape[1]
    gdim = width // groups
    starts = range(0, x_ref.shape[0], SUB_TILE)
    zs = [jnp.dot(_rms(x_ref[r0:r0 + SUB_TILE, :], g_ref[...]).astype(BF16), w_ref[...],
                  preferred_element_type=F32) for r0 in starts]
    for r0, z in zip(starts, zs):
        rows = slice(r0, r0 + SUB_TILE)
        half_z = 0.5 * z
        z = half_z + half_z * jnp.tanh(z * (GELU_A + (GELU_A * GELU_B) * (z * z)))
        zv = _rms(z[:, width:], ng_ref[...])
        if want_zv:
            zv_ref[rows, :] = zv
        zvb_scr[rows, :] = zv.astype(BF16)
        for ch in range(SUB_TILE // SGU_CHUNK):
            rs = slice(ch * SGU_CHUNK, (ch + 1) * SGU_CHUNK)
            out_rs = slice(r0 + ch * SGU_CHUNK, r0 + (ch + 1) * SGU_CHUNK)
            for gi in range(groups):
                cs = slice(gi * gdim, (gi + 1) * gdim)
                mix = jnp.dot(ws_ref[gi], zvb_scr[out_rs, cs], preferred_element_type=F32) + bias_ref[:, cs]
                o_ref[out_rs, cs] = (z[rs, cs] * mix).astype(BF16)


def _sgu(x, g, w_in, norm_g, ws_mat, bias_full, want_zv):
    rows, d = x.shape
    width = norm_g.shape[0]
    groups = ws_mat.shape[0]
    tm = ROW_TILE
    row_spec = lambda wd: pl.BlockSpec((tm, wd), lambda i: (i, 0))
    out_specs = [row_spec(width)]
    out_shape = [jax.ShapeDtypeStruct((rows, width), BF16)]
    if want_zv:
        out_specs.append(row_spec(width))
        out_shape.append(jax.ShapeDtypeStruct((rows, width), F32))
    est = (2 * tm * d * 4 + w_in.size * 2 + ws_mat.size * 2 + bias_full.size * 4
           + 2 * tm * width * 2 + (2 * tm * width * 4 if want_zv else 0)
           + tm * width * 2 + 3 * tm * 2 * width * 4)
    res = pl.pallas_call(
        functools.partial(_sgu_body, groups, want_zv),
        grid=(rows // tm,),
        in_specs=[row_spec(d), _resident((1, d)), _resident(w_in.shape), _resident((1, width)),
                  _resident(ws_mat.shape), _resident(bias_full.shape)],
        out_specs=out_specs,
        out_shape=out_shape,
        scratch_shapes=[pltpu.VMEM((tm, width), BF16)],
        compiler_params=pltpu.CompilerParams(
            dimension_semantics=("arbitrary",), vmem_limit_bytes=_vmem_limit(est)),
        name="spatial_gating",
    )(x, g.reshape(1, d), w_in, norm_g.reshape(1, width), ws_mat, bias_full)
    return res if want_zv else (res[0], None)


def _prep_ffn(w_in, w_down):
    return w_in.astype(BF16), w_down.astype(BF16)


def _sgu_spatial(w_s, b_s, seq_len):
    groups = w_s.shape[0]
    span = min(seq_len, SGU_CHUNK)
    idx = jnp.arange(span) // SGU_STREAM_CHUNK
    w = jnp.where((idx[None, :] <= idx[:, None])[None], w_s[:, :span, :span], 0.0)
    reps = SGU_CHUNK // span
    if reps > 1:
        w = jnp.einsum('ab,gts->gatbs', jnp.eye(reps, dtype=w.dtype), w).reshape(groups, SGU_CHUNK, SGU_CHUNK)
    bias = jnp.tile(b_s[:, :span], (1, reps))
    return w.astype(BF16), bias


def kernel(x_prompt, x_sample, cache_k, cache_v, cache_logf, state_pool, norm_g, ffn_w_in, ffn_w_down,
           even_w_in, even_b_f, pool_w, pool_scale, even_w_out, sgu_w_in, sgu_norm_g, sgu_w_s, sgu_b_s,
           sgu_w_out, final_g):
    b, s, d = x_prompt.shape
    bs, t, _ = x_sample.shape
    depth = norm_g.shape[0]
    heads = even_b_f.shape[1]
    width = heads * FOX_HEAD_DIM
    pool_width = pool_scale.shape[1]
    n_groups = pool_w.shape[1]
    past = cache_k.shape[2]
    sgu_width = sgu_norm_g.shape[1]
    sgu_groups = sgu_w_s.shape[1]
    sgu_gdim = sgu_width // sgu_groups

    xp = x_prompt.reshape(b * s, d)
    xs = x_sample.reshape(bs * t, d)
    kp_l, vp_l, fp_l, up_l = [], [], [], []
    ks_l, vs_l, fs_l, us_l, zs_l = [], [], [], [], []

    for l in range(depth):
        win_a, wdown_a = _prep_ffn(ffn_w_in[l, 0], ffn_w_down[l, 0])
        win_b, wdown_b = _prep_ffn(ffn_w_in[l, 1], ffn_w_down[l, 1])
        last = final_g if l == depth - 1 else None
        xp = _ffn(xp, [], norm_g[l, 0], win_a, wdown_a)
        xs = _ffn(xs, [], norm_g[l, 0], win_a, wdown_a)
        if l % 2 == 0:
            e = l // 2
            w_e = even_w_in[e]
            f_cols = _replicate_gates(w_e[:, 3 * width:3 * width + heads], heads)
            w_all = jnp.concatenate([w_e[:, :3 * width], w_e[:, 3 * width + heads:], f_cols], axis=1).astype(BF16)
            bf_pad = _replicate_gates(even_b_f[e], heads).reshape(1, V7X_LANES)
            pw = pool_w[e].astype(BF16)
            psc = pool_scale[e].reshape(n_groups, 1, pool_width // n_groups)
            w_out_att = even_w_out[e, :width].astype(BF16)
            w_out_pool = even_w_out[e, width:].astype(BF16)

            q, kb, vb, k, v, u, lf, lfw = _even_proj(xp, norm_g[l, 1], w_all, bf_pad, width, heads)
            hist = jnp.zeros((b, POOL_HIST + 1, pool_width), F32)
            att, pool = _attention_pool(heads, q.reshape(b, s, width), kb.reshape(b, s, width),
                                        vb.reshape(b, s, width), lfw.reshape(b, s, V7X_LANES),
                                        u.reshape(b, s, pool_width), hist, pw, psc)
            xp = _ffn(xp, [(att.reshape(b * s, width), w_out_att), (pool.reshape(b * s, pool_width), w_out_pool)],
                      norm_g[l, 2], win_b, wdown_b, last)
            kp_l.append(k.reshape(b, s, heads, FOX_HEAD_DIM))
            vp_l.append(v.reshape(b, s, heads, FOX_HEAD_DIM))
            fp_l.append(lf.reshape(b, s, heads))
            up_l.append(u.reshape(b, s, pool_width)[:, s - POOL_HIST:])

            q, kb, vb, k, v, u, lf, lfw = _even_proj(xs, norm_g[l, 1], w_all, bf_pad, width, heads)
            hist = jnp.pad(state_pool[e], ((0, 0), (1, 0), (0, 0)))
            channel_major = lambda c: jnp.transpose(c, (0, 2, 3, 1)).reshape(bs, width, past)
            past_kvf = (channel_major(cache_k[e]), channel_major(cache_v[e]),
                        _replicate_gates(cache_logf[e], heads))
            att, pool = _attention_pool(heads, q.reshape(bs, t, width), kb.reshape(bs, t, width),
                                        vb.reshape(bs, t, width), lfw.reshape(bs, t, V7X_LANES),
                                        u.reshape(bs, t, pool_width), hist, pw, psc, past_kvf)
            xs = _ffn(xs, [(att.reshape(bs * t, width), w_out_att), (pool.reshape(bs * t, pool_width), w_out_pool)],
                      norm_g[l, 2], win_b, wdown_b, last)
            ks_l.append(k.reshape(bs, t, heads, FOX_HEAD_DIM))
            vs_l.append(v.reshape(bs, t, heads, FOX_HEAD_DIM))
            fs_l.append(lf.reshape(bs, t, heads))
            u_ext = jnp.concatenate([state_pool[e], u.reshape(bs, t, pool_width)], axis=1)
            us_l.append(u_ext[:, -POOL_HIST:])
        else:
            o = l // 2
            w_in = sgu_w_in[o].astype(BF16)
            w_out = sgu_w_out[o].astype(BF16)
            ws_p, bias_p = _sgu_spatial(sgu_w_s[o], sgu_b_s[o], s)
            ws_s, bias_s = _sgu_spatial(sgu_w_s[o], sgu_b_s[o], t)
            expand = lambda bias: jnp.repeat(bias.T, sgu_gdim, axis=1)
            gated, _ = _sgu(xp, norm_g[l, 1], w_in, sgu_norm_g[o], ws_p, expand(bias_p), False)
            xp = _ffn(xp, [(gated, w_out)], norm_g[l, 2], win_b, wdown_b, last)
            gated, zv = _sgu(xs, norm_g[l, 1], w_in, sgu_norm_g[o], ws_s, expand(bias_s), True)
            xs = _ffn(xs, [(gated, w_out)], norm_g[l, 2], win_b, wdown_b, last)
            zs_l.append(zv.reshape(bs, t, sgu_width))

    return (xp.reshape(b, s, d), xs.reshape(bs, t, d),
            jnp.stack(kp_l), jnp.stack(vp_l), jnp.stack(fp_l), jnp.stack(up_l),
            jnp.stack(ks_l), jnp.stack(vs_l), jnp.stack(fs_l), jnp.stack(us_l), jnp.stack(zs_l))
```

```python
import functools
import math

import jax
import jax.numpy as jnp
from jax import lax
from jax.experimental import pallas as pl
from jax.experimental.pallas import tpu as pltpu

F32 = jnp.float32
BF16 = jnp.bfloat16

EPS = 1e-6
FOX_HEAD_DIM = 64
POOL_WINDOWS = (2, 4, 8, 16)
POOL_HIST = max(POOL_WINDOWS) - 1
SGU_STREAM_CHUNK = 64
SGU_CHUNK = 128

V7X_LANES = 128
V7X_MXU_DIM = 256
V7X_VMEM_BYTES = 64 * 1024 * 1024
VMEM_COMPILER_RESERVE = 8 * 1024 * 1024

ROW_TILE = 1024
SUB_TILE = 512
FF_CHUNK = V7X_MXU_DIM
ATT_Q_BLOCK = 128
ATT_KV_CHUNK = V7X_MXU_DIM
AUG_LANES = V7X_LANES
N_SPLIT = 3
GATE_COPIES = 2 * N_SPLIT
LOG2E = math.log2(math.e)
GELU_A = math.sqrt(2.0 / math.pi)
GELU_B = 0.044715


def _vmem_limit(est_bytes):
    return int(min(max(est_bytes, 16 * 1024 * 1024), V7X_VMEM_BYTES - VMEM_COMPILER_RESERVE))


def _resident(shape):
    nd = len(shape)
    return pl.BlockSpec(shape, lambda *_: (0,) * nd, pipeline_mode=pl.Buffered(1))


def _rms(x, g):
    ms = jnp.mean(x * x, axis=-1, keepdims=True)
    return x * lax.rsqrt(ms + EPS) * g


def _round_up(n, m):
    return (n + m - 1) // m * m


def _ffn_body(n_mix, n_chunks, has_final, *refs):
    x_ref = refs[0]
    a_refs = refs[1:1 + n_mix]
    w_refs = refs[1 + n_mix:1 + 2 * n_mix]
    g_ref, win_ref, wdown_ref = refs[1 + 2 * n_mix:4 + 2 * n_mix]
    pos = 4 + 2 * n_mix
    fg_ref = refs[pos] if has_final else None
    pos += int(has_final)
    o_ref, h_scr, act_scr = refs[pos:pos + 3]

    d_ff = n_chunks * FF_CHUNK
    for r0 in range(0, x_ref.shape[0], SUB_TILE):
        rows = slice(r0, r0 + SUB_TILE)
        x = x_ref[rows, :]
        for a_ref, w_ref in zip(a_refs, w_refs):
            x = x + jnp.dot(a_ref[rows, :], w_ref[...], preferred_element_type=F32)
        o_ref[rows, :] = x
        h_scr[rows, :] = _rms(x, g_ref[...]).astype(BF16)
        for j in range(n_chunks):
            cols = slice(j * FF_CHUNK, (j + 1) * FF_CHUNK)
            up_cols = slice(d_ff + j * FF_CHUNK, d_ff + (j + 1) * FF_CHUNK)
            gate = jnp.dot(h_scr[rows, :], win_ref[:, cols], preferred_element_type=F32)
            up = jnp.dot(h_scr[rows, :], win_ref[:, up_cols], preferred_element_type=F32)
            act = gate * (1.0 / (1.0 + jnp.exp(-gate))) * up
            act_scr[rows, cols] = act.astype(BF16)
        y = jnp.dot(act_scr[rows, :], wdown_ref[...], preferred_element_type=F32)
        out = o_ref[rows, :] + 0.5 * y
        if has_final:
            out = _rms(out, fg_ref[...])
        o_ref[rows, :] = out


def _ffn(x, mixes, g, w_in_all, w_down_all, which, final_g=None):
    rows, d = x.shape
    d_ff = w_down_all.shape[2]
    assert w_in_all.shape[2:] == (d, 2 * d_ff) and d_ff % FF_CHUNK == 0
    n_chunks = d_ff // FF_CHUNK
    picked = lambda shape: pl.BlockSpec((None, None) + shape, lambda i: which + (0, 0),
                                        pipeline_mode=pl.Buffered(1))
    two_chunk = 2 * FF_CHUNK
    tm = ROW_TILE
    n_mix = len(mixes)
    has_final = final_g is not None

    row_spec = lambda width: pl.BlockSpec((tm, width), lambda i: (i, 0))
    in_specs = [row_spec(d)]
    in_specs += [row_spec(a.shape[1]) for a, _ in mixes]
    in_specs += [_resident(w.shape) for _, w in mixes]
    in_specs += [_resident((1, d)), picked((d, 2 * d_ff)), picked((d_ff, d))]
    args = [x] + [a for a, _ in mixes] + [w for _, w in mixes] + [g.reshape(1, d), w_in_all, w_down_all]
    if has_final:
        in_specs.append(_resident((1, d)))
        args.append(final_g.reshape(1, d))

    est = (4 * tm * d * 4
           + sum(2 * tm * a.shape[1] * 2 + w.size * 2 for a, w in mixes)
           + 3 * d * d_ff * 2
           + tm * d * 2 + tm * d_ff * 2
           + SUB_TILE * two_chunk * 4 * 2 + SUB_TILE * d * 4 * 2)
    return pl.pallas_call(
        functools.partial(_ffn_body, n_mix, n_chunks, has_final),
        grid=(rows // tm,),
        in_specs=in_specs,
        out_specs=row_spec(d),
        out_shape=jax.ShapeDtypeStruct((rows, d), F32),
        scratch_shapes=[pltpu.VMEM((tm, d), BF16), pltpu.VMEM((tm, d_ff), BF16)],
        compiler_params=pltpu.CompilerParams(
            dimension_semantics=("arbitrary",), vmem_limit_bytes=_vmem_limit(est)),
        name="ffn_half_step",
    )(*args)


def _even_proj_body(width, heads, x_ref, g_ref, w_ref, bf_ref,
                    q_ref, kb_ref, vb_ref, k4_ref, v4_ref, u_ref, lf_ref, lfw_ref):
    starts = range(0, x_ref.shape[0], SUB_TILE)
    zs = [jnp.dot(_rms(x_ref[r0:r0 + SUB_TILE, :], g_ref[...]).astype(BF16), w_ref[...],
                  preferred_element_type=F32) for r0 in starts]
    for r0, z in zip(starts, zs):
        rows = slice(r0, r0 + SUB_TILE)
        q_ref[rows, :] = (z[:, :width] * (FOX_HEAD_DIM ** -0.5 * LOG2E)).astype(BF16)
        kb_ref[rows, :] = z[:, width:2 * width].astype(BF16)
        vb_ref[rows, :] = z[:, 2 * width:3 * width].astype(BF16)
        for hd in range(heads):
            head_rows = pl.ds(r0 * heads + hd, SUB_TILE, stride=heads)
            k4_ref[head_rows, :] = z[:, width + hd * FOX_HEAD_DIM:width + (hd + 1) * FOX_HEAD_DIM]
            v4_ref[head_rows, :] = z[:, 2 * width + hd * FOX_HEAD_DIM:2 * width + (hd + 1) * FOX_HEAD_DIM]
        u_ref[rows, :] = z[:, 3 * width:4 * width]
        f = z[:, 4 * width:4 * width + V7X_LANES] + bf_ref[...]
        logf = jnp.minimum(f, 0.0) - jnp.log(1.0 + jnp.exp(-jnp.abs(f)))
        lf_ref[:, rows] = logf.T[:heads, :]
        lfw_ref[rows, :] = logf


def _replicate_gates(cols, heads, axis=-1):
    axis = axis % cols.ndim
    src = jnp.array([h for h in range(heads) for _ in range(GATE_COPIES)])
    pad_shape = cols.shape[:axis] + (V7X_LANES - heads * (1 + GATE_COPIES),) + cols.shape[axis + 1:]
    return jnp.concatenate([cols, jnp.take(cols, src, axis=axis), jnp.zeros(pad_shape, cols.dtype)], axis=axis)


def _even_proj(x, g, w_all, bf_pad, width, heads):
    rows, d = x.shape
    tm = ROW_TILE
    n_out = w_all.shape[1]
    row_spec = lambda wd: pl.BlockSpec((tm, wd), lambda i: (i, 0))
    head_rows = pl.BlockSpec((tm * heads, FOX_HEAD_DIM), lambda i: (i, 0))
    est = (2 * tm * d * 4 + w_all.size * 2 + 2 * tm * width * (3 * 2 + 4)
           + 2 * 2 * tm * heads * V7X_LANES * 4
           + 4 * tm * V7X_LANES * 4 + tm * n_out * 4 * 2 + tm * d * 2)
    return pl.pallas_call(
        functools.partial(_even_proj_body, width, heads),
        grid=(rows // tm,),
        in_specs=[row_spec(d), _resident((1, d)), _resident(w_all.shape), _resident((1, V7X_LANES))],
        out_specs=[row_spec(width), row_spec(width), row_spec(width), head_rows, head_rows,
                   row_spec(width), pl.BlockSpec((heads, tm), lambda i: (0, i)), row_spec(V7X_LANES)],
        out_shape=[jax.ShapeDtypeStruct((rows, width), BF16),
                   jax.ShapeDtypeStruct((rows, width), BF16),
                   jax.ShapeDtypeStruct((rows, width), BF16),
                   jax.ShapeDtypeStruct((rows * heads, FOX_HEAD_DIM), F32),
                   jax.ShapeDtypeStruct((rows * heads, FOX_HEAD_DIM), F32),
                   jax.ShapeDtypeStruct((rows, width), F32),
                   jax.ShapeDtypeStruct((heads, rows), F32),
                   jax.ShapeDtypeStruct((rows, V7X_LANES), F32)],
        compiler_params=pltpu.CompilerParams(
            dimension_semantics=("arbitrary",), vmem_limit_bytes=_vmem_limit(est)),
        name="even_projection",
    )(x, g.reshape(1, d), w_all, bf_pad)


def _attn_body(past, t_new, tq, heads, *refs):
    n_all = past + t_new
    n_pad = _round_up(n_all, V7X_LANES)
    if past:
        (q_ref, k_ref, v_ref, lf_ref, kp_ref, vp_ref, lfp_ref, u_ref, hist_ref, pw_ref, ps_ref,
         att_ref, pool_ref, ca, cb, c_hi, c_mid, c_lo, kcat, qcat, vcat, pa, pb) = refs
    else:
        (q_ref, k_ref, v_ref, lf_ref, u_ref, hist_ref, pw_ref, ps_ref,
         att_ref, pool_ref, ca, cb, c_hi, c_mid, c_lo, kcat, qcat, vcat, pa, pb) = refs
    pair = pl.program_id(1)

    @pl.when(pair == 0)
    def _():
        pad = 8
        ca[0:pad, :] = jnp.zeros((pad, V7X_LANES), F32)
        cb[0:pad, :] = jnp.zeros((pad, V7X_LANES), F32)
        if past:
            ca[pad:pad + past, :] = lfp_ref[...].T
        ca[pad + past:pad + n_all, :] = lf_ref[...]
        src, dst = ca, cb
        s = 1
        while s < n_all:
            if s < pad:
                dst[pad:pad + n_all, :] = src[pad:pad + n_all, :] + src[pad - s:pad + n_all - s, :]
            else:
                dst[pad:pad + s, :] = src[pad:pad + s, :]
                dst[pad + s:pad + n_all, :] = src[pad + s:pad + n_all, :] + src[pad:pad + n_all - s, :]
            src, dst = dst, src
            s *= 2
        c = src[pad:pad + n_all, :] * LOG2E
        hi = c.astype(BF16)
        r1 = c - hi.astype(F32)
        mid = r1.astype(BF16)
        c_hi[...] = hi
        c_mid[...] = mid
        c_lo[...] = (r1 - mid.astype(F32)).astype(BF16)
        vcat[0:n_all, V7X_LANES:2 * V7X_LANES] = jnp.ones((n_all, V7X_LANES), BF16)
        if n_pad > n_all:
            kcat[n_all:n_pad, :] = jnp.zeros((n_pad - n_all, V7X_LANES + AUG_LANES), BF16)
            vcat[n_all:n_pad, :] = jnp.zeros((n_pad - n_all, 2 * V7X_LANES), BF16)

    ln = lax.broadcasted_iota(jnp.int32, (1, AUG_LANES), 1)
    base = heads + 2 * GATE_COPIES * pair

    def pat(lanes, value=1.0):
        out = jnp.zeros((1, AUG_LANES), F32)
        for l in lanes:
            out = jnp.where(ln == base + l, value, out)
        return out.astype(BF16)

    k_lanes = lambda j: [hh * GATE_COPIES + N_SPLIT + j for hh in range(2)]
    aug_k = (c_hi[...] * pat(k_lanes(0), -1.0) + c_mid[...] * pat(k_lanes(1), -1.0)
             + c_lo[...] * pat(k_lanes(2), -1.0)
             + pat([hh * GATE_COPIES + j for hh in range(2) for j in range(N_SPLIT)]))
    if past:
        kcat[0:past, 0:V7X_LANES] = kp_ref[...].T.astype(BF16)
        vcat[0:past, 0:V7X_LANES] = vp_ref[...].T.astype(BF16)
    kcat[past:n_all, 0:V7X_LANES] = k_ref[...]
    vcat[past:n_all, 0:V7X_LANES] = v_ref[...]
    kcat[0:n_all, V7X_LANES:V7X_LANES + AUG_LANES] = aug_k

    qv = q_ref[...]
    hi_q, mid_q, lo_q = c_hi[past:n_all, :], c_mid[past:n_all, :], c_lo[past:n_all, :]
    for hh in range(2):
        lo_lane = hh * GATE_COPIES
        aug_q = (hi_q * pat([lo_lane]) + mid_q * pat([lo_lane + 1]) + lo_q * pat([lo_lane + 2])
                 + pat(range(lo_lane + N_SPLIT, lo_lane + GATE_COPIES)))
        in_head = (ln >= hh * FOX_HEAD_DIM) & (ln < (hh + 1) * FOX_HEAD_DIM)
        q_h = qv * jnp.where(in_head, 1.0, 0.0).astype(BF16)
        for i in range(t_new // tq):
            qcat[i, hh * tq:(hh + 1) * tq, 0:V7X_LANES] = q_h[i * tq:(i + 1) * tq]
            qcat[i, hh * tq:(hh + 1) * tq, V7X_LANES:V7X_LANES + AUG_LANES] = aug_q[i * tq:(i + 1) * tq]

    lead = 2 * (POOL_HIST + 1)
    half = POOL_HIST + 1
    pa[0:half, :] = jnp.zeros((half, V7X_LANES), F32)
    pb[0:half, :] = jnp.zeros((half, V7X_LANES), F32)
    pa[half:lead, :] = hist_ref[...]
    pa[lead:lead + t_new, :] = u_ref[...]
    src, dst = pa, pb
    win = None
    shift = 1
    for gi in range(len(POOL_WINDOWS)):
        dst[half:lead + t_new, :] = src[half:lead + t_new, :] + src[half - shift:lead + t_new - shift, :]
        cur = dst[lead:lead + t_new, :]
        win = cur if win is None else jnp.where(pair >= gi, cur, win)
        src, dst = dst, src
        shift *= 2
    width = jnp.left_shift(2, pair)
    posn = lax.broadcasted_iota(jnp.int32, (t_new, V7X_LANES), 0) + past
    cnt = jnp.minimum(width, posn + 1).astype(F32)
    pool_dlt = (win / cnt - u_ref[...]).astype(BF16)

    nt = (((1,), (1,)), ((), ()))
    lane_o = lax.broadcasted_iota(jnp.int32, (tq, V7X_LANES), 1)
    for i in range(t_new // tq):
        r0 = i * tq
        kend = _round_up(past + r0 + tq, V7X_LANES)
        d0 = (past + r0) // V7X_LANES * V7X_LANES
        qs = qcat[i]
        bounds = [(c0, min(c0 + ATT_KV_CHUNK, kend)) for c0 in range(0, kend, ATT_KV_CHUNK)]
        s_list = []
        mx = None
        for c0, c1 in bounds:
            s = lax.dot_general(qs, kcat[c0:c1, :], nt, preferred_element_type=F32)
            if c1 > d0:
                qpos = jnp.bitwise_and(lax.broadcasted_iota(jnp.int32, (2 * tq, c1 - c0), 0), tq - 1) + (past + r0)
                kpos = lax.broadcasted_iota(jnp.int32, (2 * tq, c1 - c0), 1) + c0
                s = jnp.where(kpos <= qpos, s, -jnp.inf)
            s_list.append(s)
            fold = s[:, 0:V7X_LANES]
            for l0 in range(V7X_LANES, c1 - c0, V7X_LANES):
                fold = jnp.maximum(fold, s[:, l0:l0 + V7X_LANES])
            mx = fold if mx is None else jnp.maximum(mx, fold)
        m = jnp.max(mx, axis=1, keepdims=True)
        o = None
        for s, (c0, c1) in zip(s_list, bounds):
            pv = jnp.dot(jnp.exp2(s - m).astype(BF16), vcat[c0:c1, :], preferred_element_type=F32)
            o = pv if o is None else o + pv
        o = o[:, 0:V7X_LANES] / o[:, V7X_LANES:2 * V7X_LANES]
        att_ref[r0:r0 + tq, :] = jnp.where(lane_o < FOX_HEAD_DIM, o[0:tq], o[tq:2 * tq]).astype(BF16)

    y = jnp.dot(pool_dlt, pw_ref[...], preferred_element_type=F32)
    pool_ref[...] = (y * ps_ref[...]).astype(BF16)


def _attention_pool(heads, q, k, v, logf_wide, u, hist16, pool_w, pool_scale, past_kvf=None):
    b, t_new, width = q.shape
    n_pairs = width // V7X_LANES
    assert heads + heads * GATE_COPIES <= AUG_LANES
    past = 0 if past_kvf is None else past_kvf[0].shape[2]
    n_all = past + t_new
    n_pad = _round_up(n_all, V7X_LANES)
    tq = min(ATT_Q_BLOCK, t_new)
    assert tq & (tq - 1) == 0 and t_new % tq == 0 and heads % 2 == 0

    slab = lambda rows: pl.BlockSpec((None, rows, V7X_LANES), lambda bi, pi: (bi, 0, pi))
    gates = lambda rows: pl.BlockSpec((None, rows, V7X_LANES), lambda bi, pi: (bi, 0, 0))
    in_specs = [slab(t_new), slab(t_new), slab(t_new), gates(t_new)]
    args = [q, k, v, logf_wide]
    if past:
        channel_major = pl.BlockSpec((None, V7X_LANES, past), lambda bi, pi: (bi, pi, 0))
        in_specs += [channel_major, channel_major,
                     pl.BlockSpec((None, V7X_LANES, past), lambda bi, pi: (bi, 0, 0))]
        args += list(past_kvf)
    in_specs += [slab(t_new), slab(POOL_HIST + 1),
                 pl.BlockSpec((None, V7X_LANES, V7X_LANES), lambda bi, pi: (pi, 0, 0)),
                 pl.BlockSpec((None, 1, V7X_LANES), lambda bi, pi: (pi, 0, 0))]
    args += [u, hist16, pool_w, pool_scale]

    lane_pad_f32 = V7X_LANES * 4
    scratch = [
        pltpu.VMEM((8 + n_all, V7X_LANES), F32), pltpu.VMEM((8 + n_all, V7X_LANES), F32),
        pltpu.VMEM((n_all, AUG_LANES), BF16), pltpu.VMEM((n_all, AUG_LANES), BF16),
        pltpu.VMEM((n_all, AUG_LANES), BF16),
        pltpu.VMEM((n_pad, V7X_LANES + AUG_LANES), BF16),
        pltpu.VMEM((t_new // tq, 2 * tq, V7X_LANES + AUG_LANES), BF16),
        pltpu.VMEM((n_pad, 2 * V7X_LANES), BF16),
        pltpu.VMEM((2 * (POOL_HIST + 1) + t_new, V7X_LANES), F32),
        pltpu.VMEM((2 * (POOL_HIST + 1) + t_new, V7X_LANES), F32),
    ]
    est = (2 * t_new * V7X_LANES * (2 + 2 + 2 + 4 + 2 + 2) + 2 * t_new * lane_pad_f32
           + 2 * past * (2 * V7X_LANES * 4 + lane_pad_f32)
           + 3 * (8 + n_all) * lane_pad_f32 + n_all * lane_pad_f32
           + 2 * n_pad * (V7X_LANES + AUG_LANES) * 2 + 2 * t_new * (V7X_LANES + AUG_LANES) * 2
           + 2 * (32 + t_new) * lane_pad_f32
           + 4 * 2 * tq * n_pad * 4 + 10 * n_all * lane_pad_f32)
    return pl.pallas_call(
        functools.partial(_attn_body, past, t_new, tq, heads),
        grid=(b, n_pairs),
        in_specs=in_specs,
        out_specs=[slab(t_new), slab(t_new)],
        out_shape=[jax.ShapeDtypeStruct((b, t_new, width), BF16),
                   jax.ShapeDtypeStruct((b, t_new, width), BF16)],
        scratch_shapes=scratch,
        compiler_params=pltpu.CompilerParams(
            dimension_semantics=("arbitrary", "arbitrary"), vmem_limit_bytes=_vmem_limit(est)),
        name="fox_attention_pool",
    )(*args)


def _sgu_body(groups, want_zv, x_ref, g_ref, w_ref, ng_ref, ws_ref, bias_ref, *rest):
    if want_zv:
        o_ref, zv_ref, zvb_scr = rest
    else:
        o_ref, zvb_scr = rest
        zv_ref = None
    width = ng_ref.shape[1]
    gdim = width // groups
    starts = range(0, x_ref.shape[0], SUB_TILE)
    zs = [jnp.dot(_rms(x_ref[r0:r0 + SUB_TILE, :], g_ref[...]).astype(BF16), w_ref[...],
                  preferred_element_type=F32) for r0 in starts]
    for r0, z in zip(starts, zs):
        rows = slice(r0, r0 + SUB_TILE)
        half_z = 0.5 * z
        z = half_z + half_z * jnp.tanh(z * (GELU_A + (GELU_A * GELU_B) * (z * z)))
        zv = _rms(z[:, width:], ng_ref[...])
        if want_zv:
            zv_ref[rows, :] = zv
        zvb_scr[rows, :] = zv.astype(BF16)
        for ch in range(SUB_TILE // SGU_CHUNK):
            rs = slice(ch * SGU_CHUNK, (ch + 1) * SGU_CHUNK)
            out_rs = slice(r0 + ch * SGU_CHUNK, r0 + (ch + 1) * SGU_CHUNK)
            for gi in range(groups):
                cs = slice(gi * gdim, (gi + 1) * gdim)
                mix = jnp.dot(ws_ref[gi], zvb_scr[out_rs, cs], preferred_element_type=F32) + bias_ref[:, cs]
                o_ref[out_rs, cs] = (z[rs, cs] * mix).astype(BF16)


def _sgu(x, g, w_in, norm_g, ws_mat, bias_full, want_zv):
    rows, d = x.shape
    width = norm_g.shape[0]
    groups = ws_mat.shape[0]
    tm = ROW_TILE
    row_spec = lambda wd: pl.BlockSpec((tm, wd), lambda i: (i, 0))
    out_specs = [row_spec(width)]
    out_shape = [jax.ShapeDtypeStruct((rows, width), BF16)]
    if want_zv:
        out_specs.append(row_spec(width))
        out_shape.append(jax.ShapeDtypeStruct((rows, width), F32))
    est = (2 * tm * d * 4 + w_in.size * 2 + ws_mat.size * 2 + bias_full.size * 4
           + 2 * tm * width * 2 + (2 * tm * width * 4 if want_zv else 0)
           + tm * width * 2 + 3 * tm * 2 * width * 4)
    res = pl.pallas_call(
        functools.partial(_sgu_body, groups, want_zv),
        grid=(rows // tm,),
        in_specs=[row_spec(d), _resident((1, d)), _resident(w_in.shape), _resident((1, width)),
                  _resident(ws_mat.shape), _resident(bias_full.shape)],
        out_specs=out_specs,
        out_shape=out_shape,
        scratch_shapes=[pltpu.VMEM((tm, width), BF16)],
        compiler_params=pltpu.CompilerParams(
            dimension_semantics=("arbitrary",), vmem_limit_bytes=_vmem_limit(est)),
        name="spatial_gating",
    )(x, g.reshape(1, d), w_in, norm_g.reshape(1, width), ws_mat, bias_full)
    return res if want_zv else (res[0], None)


def _sgu_spatial(w_s, b_s, seq_len):
    groups = w_s.shape[0]
    span = min(seq_len, SGU_CHUNK)
    idx = jnp.arange(span) // SGU_STREAM_CHUNK
    w = jnp.where((idx[None, :] <= idx[:, None])[None], w_s[:, :span, :span], 0.0)
    reps = SGU_CHUNK // span
    if reps > 1:
        w = jnp.einsum('ab,gts->gatbs', jnp.eye(reps, dtype=w.dtype), w).reshape(groups, SGU_CHUNK, SGU_CHUNK)
    bias = jnp.tile(b_s[:, :span], (1, reps))
    return w.astype(BF16), bias


def kernel(x_prompt, x_sample, cache_k, cache_v, cache_logf, state_pool, norm_g, ffn_w_in, ffn_w_down,
           even_w_in, even_b_f, pool_w, pool_scale, even_w_out, sgu_w_in, sgu_norm_g, sgu_w_s, sgu_b_s,
           sgu_w_out, final_g):
    b, s, d = x_prompt.shape
    bs, t, _ = x_sample.shape
    depth = norm_g.shape[0]
    heads = even_b_f.shape[1]
    width = heads * FOX_HEAD_DIM
    pool_width = pool_scale.shape[1]
    n_groups = pool_w.shape[1]
    past = cache_k.shape[2]
    sgu_width = sgu_norm_g.shape[1]
    sgu_groups = sgu_w_s.shape[1]
    sgu_gdim = sgu_width // sgu_groups

    xp = x_prompt.reshape(b * s, d)
    xs = x_sample.reshape(bs * t, d)
    kp_l, vp_l, fp_l, up_l = [], [], [], []
    ks_l, vs_l, fs_l, us_l, zs_l = [], [], [], [], []

    w_in_all = ffn_w_in.astype(BF16)
    w_down_all = ffn_w_down.astype(BF16)
    for l in range(depth):
        last = final_g if l == depth - 1 else None
        ffn_b = functools.partial(_ffn, g=norm_g[l, 2], w_in_all=w_in_all, w_down_all=w_down_all,
                                  which=(l, 1), final_g=last)
        xp = _ffn(xp, [], norm_g[l, 0], w_in_all, w_down_all, (l, 0))
        xs = _ffn(xs, [], norm_g[l, 0], w_in_all, w_down_all, (l, 0))
        if l % 2 == 0:
            e = l // 2
            w_e = even_w_in[e]
            f_cols = _replicate_gates(w_e[:, 3 * width:3 * width + heads], heads)
            w_all = jnp.concatenate([w_e[:, :3 * width], w_e[:, 3 * width + heads:], f_cols], axis=1).astype(BF16)
            bf_pad = _replicate_gates(even_b_f[e], heads).reshape(1, V7X_LANES)
            pw = pool_w[e].astype(BF16)
            psc = pool_scale[e].reshape(n_groups, 1, pool_width // n_groups)
            w_out_att = even_w_out[e, :width].astype(BF16)
            w_out_pool = even_w_out[e, width:].astype(BF16)

            q, kb, vb, k, v, u, lf, lfw = _even_proj(xp, norm_g[l, 1], w_all, bf_pad, width, heads)
            hist = jnp.zeros((b, POOL_HIST + 1, pool_width), F32)
            att, pool = _attention_pool(heads, q.reshape(b, s, width), kb.reshape(b, s, width),
                                        vb.reshape(b, s, width), lfw.reshape(b, s, V7X_LANES),
                                        u.reshape(b, s, pool_width), hist, pw, psc)
            xp = ffn_b(xp, [(att.reshape(b * s, width), w_out_att), (pool.reshape(b * s, pool_width), w_out_pool)])
            kp_l.append(k.reshape(b, s, heads, FOX_HEAD_DIM))
            vp_l.append(v.reshape(b, s, heads, FOX_HEAD_DIM))
            fp_l.append(jnp.transpose(lf.reshape(heads, b, s), (1, 2, 0)))
            up_l.append(u.reshape(b, s, pool_width)[:, s - POOL_HIST:])

            q, kb, vb, k, v, u, lf, lfw = _even_proj(xs, norm_g[l, 1], w_all, bf_pad, width, heads)
            hist = jnp.pad(state_pool[e], ((0, 0), (1, 0), (0, 0)))
            channel_major = lambda c: jnp.transpose(c, (0, 2, 3, 1)).reshape(bs, width, past)
            past_kvf = (channel_major(cache_k[e]), channel_major(cache_v[e]),
                        _replicate_gates(jnp.transpose(cache_logf[e], (0, 2, 1)), heads, axis=1))
            att, pool = _attention_pool(heads, q.reshape(bs, t, width), kb.reshape(bs, t, width),
                                        vb.reshape(bs, t, width), lfw.reshape(bs, t, V7X_LANES),
                                        u.reshape(bs, t, pool_width), hist, pw, psc, past_kvf)
            xs = ffn_b(xs, [(att.reshape(bs * t, width), w_out_att), (pool.reshape(bs * t, pool_width), w_out_pool)])
            ks_l.append(k.reshape(bs, t, heads, FOX_HEAD_DIM))
            vs_l.append(v.reshape(bs, t, heads, FOX_HEAD_DIM))
            fs_l.append(jnp.transpose(lf.reshape(heads, bs, t), (1, 2, 0)))
            u_ext = jnp.concatenate([state_pool[e], u.reshape(bs, t, pool_width)], axis=1)
            us_l.append(u_ext[:, -POOL_HIST:])
        else:
            o = l // 2
            w_in = sgu_w_in[o].astype(BF16)
            w_out = sgu_w_out[o].astype(BF16)
            ws_p, bias_p = _sgu_spatial(sgu_w_s[o], sgu_b_s[o], s)
            ws_s, bias_s = _sgu_spatial(sgu_w_s[o], sgu_b_s[o], t)
            expand = lambda bias: jnp.repeat(bias.T, sgu_gdim, axis=1)
            gated, _ = _sgu(xp, norm_g[l, 1], w_in, sgu_norm_g[o], ws_p, expand(bias_p), False)
            xp = ffn_b(xp, [(gated, w_out)])
            gated, zv = _sgu(xs, norm_g[l, 1], w_in, sgu_norm_g[o], ws_s, expand(bias_s), True)
            xs = ffn_b(xs, [(gated, w_out)])
            zs_l.append(zv.reshape(bs, t, sgu_width))

    return (xp.reshape(b, s, d), xs.reshape(bs, t, d),
            jnp.stack(kp_l), jnp.stack(vp_l), jnp.stack(fp_l), jnp.stack(up_l),
            jnp.stack(ks_l), jnp.stack(vs_l), jnp.stack(fs_l), jnp.stack(us_l), jnp.stack(zs_l))
```

```python
import functools
import math

import jax
import jax.numpy as jnp
from jax import lax
from jax.experimental import pallas as pl
from jax.experimental.pallas import tpu as pltpu

F32 = jnp.float32
BF16 = jnp.bfloat16

EPS = 1e-6
FOX_HEAD_DIM = 64
POOL_WINDOWS = (2, 4, 8, 16)
POOL_HIST = max(POOL_WINDOWS) - 1
SGU_STREAM_CHUNK = 64
SGU_CHUNK = 128

V7X_LANES = 128
V7X_MXU_DIM = 256
V7X_VMEM_BYTES = 64 * 1024 * 1024
VMEM_COMPILER_RESERVE = 8 * 1024 * 1024

ROW_TILE = 1024
SUB_TILE = 512
FF_CHUNK = V7X_MXU_DIM
ATT_Q_BLOCK = 256
ATT_KV_CHUNK = V7X_MXU_DIM
AUG_LANES = V7X_LANES
N_SPLIT = 3
GATE_COPIES = 2 * N_SPLIT
LOG2E = math.log2(math.e)
GELU_A = math.sqrt(2.0 / math.pi)
GELU_B = 0.044715


def _vmem_limit(est_bytes):
    return int(min(max(est_bytes, 16 * 1024 * 1024), V7X_VMEM_BYTES - VMEM_COMPILER_RESERVE))


def _resident(shape):
    nd = len(shape)
    return pl.BlockSpec(shape, lambda *_: (0,) * nd, pipeline_mode=pl.Buffered(1))


def _rms(x, g):
    ms = jnp.mean(x * x, axis=-1, keepdims=True)
    return x * lax.rsqrt(ms + EPS) * g


def _round_up(n, m):
    return (n + m - 1) // m * m


def _ffn_body(n_mix, n_chunks, has_final, *refs):
    x_ref = refs[0]
    a_refs = refs[1:1 + n_mix]
    w_refs = refs[1 + n_mix:1 + 2 * n_mix]
    g_ref, win_ref, wdown_ref = refs[1 + 2 * n_mix:4 + 2 * n_mix]
    pos = 4 + 2 * n_mix
    fg_ref = refs[pos] if has_final else None
    pos += int(has_final)
    o_ref, h_scr, act_scr = refs[pos:pos + 3]

    d_ff = n_chunks * FF_CHUNK
    for r0 in range(0, x_ref.shape[0], SUB_TILE):
        rows = slice(r0, r0 + SUB_TILE)
        x = x_ref[rows, :]
        for a_ref, w_ref in zip(a_refs, w_refs):
            x = x + jnp.dot(a_ref[rows, :], w_ref[...], preferred_element_type=F32)
        o_ref[rows, :] = x
        h_scr[rows, :] = _rms(x, g_ref[...]).astype(BF16)
        for j in range(n_chunks):
            cols = slice(j * FF_CHUNK, (j + 1) * FF_CHUNK)
            up_cols = slice(d_ff + j * FF_CHUNK, d_ff + (j + 1) * FF_CHUNK)
            gate = jnp.dot(h_scr[rows, :], win_ref[:, cols], preferred_element_type=F32)
            up = jnp.dot(h_scr[rows, :], win_ref[:, up_cols], preferred_element_type=F32)
            act = gate * (1.0 / (1.0 + jnp.exp(-gate))) * up
            act_scr[rows, cols] = act.astype(BF16)
        y = jnp.dot(act_scr[rows, :], wdown_ref[...], preferred_element_type=F32)
        out = o_ref[rows, :] + 0.5 * y
        if has_final:
            out = _rms(out, fg_ref[...])
        o_ref[rows, :] = out


def _ffn(x, mixes, g, w_in_all, w_down_all, which, final_g=None):
    rows, d = x.shape
    d_ff = w_down_all.shape[2]
    assert w_in_all.shape[2:] == (d, 2 * d_ff) and d_ff % FF_CHUNK == 0
    n_chunks = d_ff // FF_CHUNK
    picked = lambda shape: pl.BlockSpec((None, None) + shape, lambda i: which + (0, 0),
                                        pipeline_mode=pl.Buffered(1))
    two_chunk = 2 * FF_CHUNK
    tm = ROW_TILE
    n_mix = len(mixes)
    has_final = final_g is not None

    row_spec = lambda width: pl.BlockSpec((tm, width), lambda i: (i, 0))
    in_specs = [row_spec(d)]
    in_specs += [row_spec(a.shape[1]) for a, _ in mixes]
    in_specs += [_resident(w.shape) for _, w in mixes]
    in_specs += [_resident((1, d)), picked((d, 2 * d_ff)), picked((d_ff, d))]
    args = [x] + [a for a, _ in mixes] + [w for _, w in mixes] + [g.reshape(1, d), w_in_all, w_down_all]
    if has_final:
        in_specs.append(_resident((1, d)))
        args.append(final_g.reshape(1, d))

    est = (4 * tm * d * 4
           + sum(2 * tm * a.shape[1] * 2 + w.size * 2 for a, w in mixes)
           + 3 * d * d_ff * 2
           + tm * d * 2 + tm * d_ff * 2
           + SUB_TILE * two_chunk * 4 * 2 + SUB_TILE * d * 4 * 2)
    return pl.pallas_call(
        functools.partial(_ffn_body, n_mix, n_chunks, has_final),
        grid=(rows // tm,),
        in_specs=in_specs,
        out_specs=row_spec(d),
        out_shape=jax.ShapeDtypeStruct((rows, d), F32),
        scratch_shapes=[pltpu.VMEM((tm, d), BF16), pltpu.VMEM((tm, d_ff), BF16)],
        compiler_params=pltpu.CompilerParams(
            dimension_semantics=("arbitrary",), vmem_limit_bytes=_vmem_limit(est)),
        name="ffn_half_step",
    )(*args)


def _even_proj_body(width, heads, x_ref, g_ref, w_ref, bf_ref,
                    q_ref, kb_ref, vb_ref, k4_ref, v4_ref, u_ref, lf_ref, lfw_ref):
    starts = range(0, x_ref.shape[0], SUB_TILE)
    zs = [jnp.dot(_rms(x_ref[r0:r0 + SUB_TILE, :], g_ref[...]).astype(BF16), w_ref[...],
                  preferred_element_type=F32) for r0 in starts]
    for r0, z in zip(starts, zs):
        rows = slice(r0, r0 + SUB_TILE)
        q_ref[rows, :] = (z[:, :width] * (FOX_HEAD_DIM ** -0.5 * LOG2E)).astype(BF16)
        kb_ref[rows, :] = z[:, width:2 * width].astype(BF16)
        vb_ref[rows, :] = z[:, 2 * width:3 * width].astype(BF16)
        for hd in range(heads):
            head_rows = pl.ds(r0 * heads + hd, SUB_TILE, stride=heads)
            k4_ref[head_rows, :] = z[:, width + hd * FOX_HEAD_DIM:width + (hd + 1) * FOX_HEAD_DIM]
            v4_ref[head_rows, :] = z[:, 2 * width + hd * FOX_HEAD_DIM:2 * width + (hd + 1) * FOX_HEAD_DIM]
        u_ref[rows, :] = z[:, 3 * width:4 * width]
        f = z[:, 4 * width:4 * width + V7X_LANES] + bf_ref[...]
        logf = jnp.minimum(f, 0.0) - jnp.log(1.0 + jnp.exp(-jnp.abs(f)))
        lf_ref[:, rows] = logf.T[:heads, :]
        lfw_ref[rows, :] = logf


def _replicate_gates(cols, heads, axis=-1):
    axis = axis % cols.ndim
    src = jnp.array([h for h in range(heads) for _ in range(GATE_COPIES)])
    pad_shape = cols.shape[:axis] + (V7X_LANES - heads * (1 + GATE_COPIES),) + cols.shape[axis + 1:]
    return jnp.concatenate([cols, jnp.take(cols, src, axis=axis), jnp.zeros(pad_shape, cols.dtype)], axis=axis)


def _even_proj(x, g, w_all, bf_pad, width, heads):
    rows, d = x.shape
    tm = ROW_TILE
    n_out = w_all.shape[1]
    row_spec = lambda wd: pl.BlockSpec((tm, wd), lambda i: (i, 0))
    head_rows = pl.BlockSpec((tm * heads, FOX_HEAD_DIM), lambda i: (i, 0))
    est = (2 * tm * d * 4 + w_all.size * 2 + 2 * tm * width * (3 * 2 + 4)
           + 2 * 2 * tm * heads * V7X_LANES * 4
           + 4 * tm * V7X_LANES * 4 + tm * n_out * 4 * 2 + tm * d * 2)
    return pl.pallas_call(
        functools.partial(_even_proj_body, width, heads),
        grid=(rows // tm,),
        in_specs=[row_spec(d), _resident((1, d)), _resident(w_all.shape), _resident((1, V7X_LANES))],
        out_specs=[row_spec(width), row_spec(width), row_spec(width), head_rows, head_rows,
                   row_spec(width), pl.BlockSpec((heads, tm), lambda i: (0, i)), row_spec(V7X_LANES)],
        out_shape=[jax.ShapeDtypeStruct((rows, width), BF16),
                   jax.ShapeDtypeStruct((rows, width), BF16),
                   jax.ShapeDtypeStruct((rows, width), BF16),
                   jax.ShapeDtypeStruct((rows * heads, FOX_HEAD_DIM), F32),
                   jax.ShapeDtypeStruct((rows * heads, FOX_HEAD_DIM), F32),
                   jax.ShapeDtypeStruct((rows, width), F32),
                   jax.ShapeDtypeStruct((heads, rows), F32),
                   jax.ShapeDtypeStruct((rows, V7X_LANES), F32)],
        compiler_params=pltpu.CompilerParams(
            dimension_semantics=("arbitrary",), vmem_limit_bytes=_vmem_limit(est)),
        name="even_projection",
    )(x, g.reshape(1, d), w_all, bf_pad)


def _attn_body(past, t_new, tq, heads, *refs):
    n_all = past + t_new
    n_pad = _round_up(n_all, V7X_LANES)
    if past:
        (q_ref, k_ref, v_ref, lf_ref, kp_ref, vp_ref, lfp_ref, u_ref, hist_ref, pw_ref, ps_ref,
         att_ref, pool_ref, ca, cb, c_hi, c_mid, c_lo, kcat, kt, qcat, vcat, pa, pb) = refs
    else:
        (q_ref, k_ref, v_ref, lf_ref, u_ref, hist_ref, pw_ref, ps_ref,
         att_ref, pool_ref, ca, cb, c_hi, c_mid, c_lo, kcat, kt, qcat, vcat, pa, pb) = refs
    pair = pl.program_id(1)

    @pl.when(pair == 0)
    def _():
        pad = 8
        ca[0:pad, :] = jnp.zeros((pad, V7X_LANES), F32)
        cb[0:pad, :] = jnp.zeros((pad, V7X_LANES), F32)
        if past:
            ca[pad:pad + past, :] = lfp_ref[...].T
        ca[pad + past:pad + n_all, :] = lf_ref[...]
        src, dst = ca, cb
        s = 1
        while s < n_all:
            if s < pad:
                dst[pad:pad + n_all, :] = src[pad:pad + n_all, :] + src[pad - s:pad + n_all - s, :]
            else:
                dst[pad:pad + s, :] = src[pad:pad + s, :]
                dst[pad + s:pad + n_all, :] = src[pad + s:pad + n_all, :] + src[pad:pad + n_all - s, :]
            src, dst = dst, src
            s *= 2
        c = src[pad:pad + n_all, :] * LOG2E
        hi = c.astype(BF16)
        r1 = c - hi.astype(F32)
        mid = r1.astype(BF16)
        c_hi[...] = hi
        c_mid[...] = mid
        c_lo[...] = (r1 - mid.astype(F32)).astype(BF16)
        vcat[0:n_all, V7X_LANES:2 * V7X_LANES] = jnp.ones((n_all, V7X_LANES), BF16)
        if n_pad > n_all:
            kcat[n_all:n_pad, :] = jnp.zeros((n_pad - n_all, V7X_LANES + AUG_LANES), BF16)
            vcat[n_all:n_pad, :] = jnp.zeros((n_pad - n_all, 2 * V7X_LANES), BF16)

    ln = lax.broadcasted_iota(jnp.int32, (1, AUG_LANES), 1)
    base = heads + 2 * GATE_COPIES * pair

    def pat(lanes, value=1.0):
        out = jnp.zeros((1, AUG_LANES), F32)
        for l in lanes:
            out = jnp.where(ln == base + l, value, out)
        return out.astype(BF16)

    k_lanes = lambda j: [hh * GATE_COPIES + N_SPLIT + j for hh in range(2)]
    aug_k = (c_hi[...] * pat(k_lanes(0), -1.0) + c_mid[...] * pat(k_lanes(1), -1.0)
             + c_lo[...] * pat(k_lanes(2), -1.0)
             + pat([hh * GATE_COPIES + j for hh in range(2) for j in range(N_SPLIT)]))
    if past:
        kcat[0:past, 0:V7X_LANES] = kp_ref[...].T.astype(BF16)
        vcat[0:past, 0:V7X_LANES] = vp_ref[...].T.astype(BF16)
    kcat[past:n_all, 0:V7X_LANES] = k_ref[...]
    vcat[past:n_all, 0:V7X_LANES] = v_ref[...]
    kcat[0:n_all, V7X_LANES:V7X_LANES + AUG_LANES] = aug_k
    kt[...] = kcat[...].T

    qv = q_ref[...]
    hi_q, mid_q, lo_q = c_hi[past:n_all, :], c_mid[past:n_all, :], c_lo[past:n_all, :]
    for hh in range(2):
        lo_lane = hh * GATE_COPIES
        aug_q = (hi_q * pat([lo_lane]) + mid_q * pat([lo_lane + 1]) + lo_q * pat([lo_lane + 2])
                 + pat(range(lo_lane + N_SPLIT, lo_lane + GATE_COPIES)))
        in_head = (ln >= hh * FOX_HEAD_DIM) & (ln < (hh + 1) * FOX_HEAD_DIM)
        q_h = qv * jnp.where(in_head, 1.0, 0.0).astype(BF16)
        for i in range(t_new // tq):
            qcat[i, hh * tq:(hh + 1) * tq, 0:V7X_LANES] = q_h[i * tq:(i + 1) * tq]
            qcat[i, hh * tq:(hh + 1) * tq, V7X_LANES:V7X_LANES + AUG_LANES] = aug_q[i * tq:(i + 1) * tq]

    def pool_group():
        lead = 2 * (POOL_HIST + 1)
        half = POOL_HIST + 1
        pa[0:half, :] = jnp.zeros((half, V7X_LANES), F32)
        pb[0:half, :] = jnp.zeros((half, V7X_LANES), F32)
        pa[half:lead, :] = hist_ref[...]
        pa[lead:lead + t_new, :] = u_ref[...]
        src, dst = pa, pb
        win = None
        shift = 1
        for gi in range(len(POOL_WINDOWS)):
            dst[half:lead + t_new, :] = src[half:lead + t_new, :] + src[half - shift:lead + t_new - shift, :]
            cur = dst[lead:lead + t_new, :]
            win = cur if win is None else jnp.where(pair >= gi, cur, win)
            src, dst = dst, src
            shift *= 2
        width = jnp.left_shift(2, pair)
        posn = lax.broadcasted_iota(jnp.int32, (t_new, V7X_LANES), 0) + past
        cnt = jnp.minimum(width, posn + 1).astype(F32)
        dlt = (win / cnt - u_ref[...]).astype(BF16)
        y = jnp.dot(dlt, pw_ref[...], preferred_element_type=F32)
        pool_ref[...] = (y * ps_ref[...]).astype(BF16)

    n_blocks = t_new // tq
    lane_o = lax.broadcasted_iota(jnp.int32, (tq, V7X_LANES), 1)
    for i in range(n_blocks):
        r0 = i * tq
        kend = _round_up(past + r0 + tq, V7X_LANES)
        d0 = (past + r0) // V7X_LANES * V7X_LANES
        qs = qcat[i]
        bounds = [(c0, min(c0 + ATT_KV_CHUNK, kend)) for c0 in range(0, kend, ATT_KV_CHUNK)]
        s_list = []
        mx = None
        for c0, c1 in bounds:
            s = jnp.dot(qs, kt[:, c0:c1], preferred_element_type=F32)
            if c1 > d0:
                qpos = jnp.bitwise_and(lax.broadcasted_iota(jnp.int32, (2 * tq, c1 - c0), 0), tq - 1) + (past + r0)
                kpos = lax.broadcasted_iota(jnp.int32, (2 * tq, c1 - c0), 1) + c0
                s = jnp.where(kpos <= qpos, s, -jnp.inf)
            s_list.append(s)
            fold = s[:, 0:V7X_LANES]
            for l0 in range(V7X_LANES, c1 - c0, V7X_LANES):
                fold = jnp.maximum(fold, s[:, l0:l0 + V7X_LANES])
            mx = fold if mx is None else jnp.maximum(mx, fold)
        m = jnp.max(mx, axis=1, keepdims=True)
        o = None
        for s, (c0, c1) in zip(s_list, bounds):
            pv = jnp.dot(jnp.exp2(s - m).astype(BF16), vcat[c0:c1, :], preferred_element_type=F32)
            o = pv if o is None else o + pv
        o = o[:, 0:V7X_LANES] / o[:, V7X_LANES:2 * V7X_LANES]
        att_ref[r0:r0 + tq, :] = jnp.where(lane_o < FOX_HEAD_DIM, o[0:tq], o[tq:2 * tq]).astype(BF16)
    pool_group()


def _attention_pool(heads, q, k, v, logf_wide, u, hist16, pool_w, pool_scale, past_kvf=None):
    b, t_new, width = q.shape
    n_pairs = width // V7X_LANES
    assert heads + heads * GATE_COPIES <= AUG_LANES
    past = 0 if past_kvf is None else past_kvf[0].shape[2]
    n_all = past + t_new
    n_pad = _round_up(n_all, V7X_LANES)
    tq = min(ATT_Q_BLOCK, t_new)
    assert tq & (tq - 1) == 0 and t_new % tq == 0 and heads % 2 == 0

    slab = lambda rows: pl.BlockSpec((None, rows, V7X_LANES), lambda bi, pi: (bi, 0, pi))
    gates = lambda rows: pl.BlockSpec((None, rows, V7X_LANES), lambda bi, pi: (bi, 0, 0))
    in_specs = [slab(t_new), slab(t_new), slab(t_new), gates(t_new)]
    args = [q, k, v, logf_wide]
    if past:
        channel_major = pl.BlockSpec((None, V7X_LANES, past), lambda bi, pi: (bi, pi, 0))
        in_specs += [channel_major, channel_major,
                     pl.BlockSpec((None, V7X_LANES, past), lambda bi, pi: (bi, 0, 0))]
        args += list(past_kvf)
    in_specs += [slab(t_new), slab(POOL_HIST + 1),
                 pl.BlockSpec((None, V7X_LANES, V7X_LANES), lambda bi, pi: (pi, 0, 0)),
                 pl.BlockSpec((None, 1, V7X_LANES), lambda bi, pi: (pi, 0, 0))]
    args += [u, hist16, pool_w, pool_scale]

    lane_pad_f32 = V7X_LANES * 4
    scratch = [
        pltpu.VMEM((8 + n_all, V7X_LANES), F32), pltpu.VMEM((8 + n_all, V7X_LANES), F32),
        pltpu.VMEM((n_all, AUG_LANES), BF16), pltpu.VMEM((n_all, AUG_LANES), BF16),
        pltpu.VMEM((n_all, AUG_LANES), BF16),
        pltpu.VMEM((n_pad, V7X_LANES + AUG_LANES), BF16),
        pltpu.VMEM((V7X_LANES + AUG_LANES, n_pad), BF16),
        pltpu.VMEM((t_new // tq, 2 * tq, V7X_LANES + AUG_LANES), BF16),
        pltpu.VMEM((n_pad, 2 * V7X_LANES), BF16),
        pltpu.VMEM((2 * (POOL_HIST + 1) + t_new, V7X_LANES), F32),
        pltpu.VMEM((2 * (POOL_HIST + 1) + t_new, V7X_LANES), F32),
    ]
    est = (2 * t_new * V7X_LANES * (2 + 2 + 2 + 4 + 2 + 2) + 2 * t_new * lane_pad_f32
           + 2 * past * (2 * V7X_LANES * 4 + lane_pad_f32)
           + 3 * (8 + n_all) * lane_pad_f32 + n_all * lane_pad_f32
           + 2 * n_pad * (V7X_LANES + AUG_LANES) * 2 + 2 * t_new * (V7X_LANES + AUG_LANES) * 2
           + 2 * (32 + t_new) * lane_pad_f32
           + 4 * 2 * tq * n_pad * 4 + 10 * n_all * lane_pad_f32)
    return pl.pallas_call(
        functools.partial(_attn_body, past, t_new, tq, heads),
        grid=(b, n_pairs),
        in_specs=in_specs,
        out_specs=[slab(t_new), slab(t_new)],
        out_shape=[jax.ShapeDtypeStruct((b, t_new, width), BF16),
                   jax.ShapeDtypeStruct((b, t_new, width), BF16)],
        scratch_shapes=scratch,
        compiler_params=pltpu.CompilerParams(
            dimension_semantics=("arbitrary", "arbitrary"), vmem_limit_bytes=_vmem_limit(est)),
        name="fox_attention_pool",
    )(*args)


def _sgu_body(groups, want_zv, x_ref, g_ref, w_ref, ng_ref, ws_ref, bias_ref, *rest):
    if want_zv:
        o_ref, zv_ref, zvb_scr = rest
    else:
        o_ref, zvb_scr = rest
        zv_ref = None
    width = ng_ref.shape[1]
    gdim = width // groups
    starts = range(0, x_ref.shape[0], SUB_TILE)
    zs = [jnp.dot(_rms(x_ref[r0:r0 + SUB_TILE, :], g_ref[...]).astype(BF16), w_ref[...],
                  preferred_element_type=F32) for r0 in starts]
    for r0, z in zip(starts, zs):
        rows = slice(r0, r0 + SUB_TILE)
        half_z = 0.5 * z
        z = half_z + half_z * jnp.tanh(z * (GELU_A + (GELU_A * GELU_B) * (z * z)))
        zv = _rms(z[:, width:], ng_ref[...])
        if want_zv:
            zv_ref[rows, :] = zv
        zvb_scr[rows, :] = zv.astype(BF16)
        for ch in range(SUB_TILE // SGU_CHUNK):
            rs = slice(ch * SGU_CHUNK, (ch + 1) * SGU_CHUNK)
            out_rs = slice(r0 + ch * SGU_CHUNK, r0 + (ch + 1) * SGU_CHUNK)
            for gi in range(groups):
                cs = slice(gi * gdim, (gi + 1) * gdim)
                mix = jnp.dot(ws_ref[gi], zvb_scr[out_rs, cs], preferred_element_type=F32) + bias_ref[:, cs]
                o_ref[out_rs, cs] = (z[rs, cs] * mix).astype(BF16)


def _sgu(x, g, w_in, norm_g, ws_mat, bias_full, want_zv):
    rows, d = x.shape
    width = norm_g.shape[0]
    groups = ws_mat.shape[0]
    tm = ROW_TILE
    row_spec = lambda wd: pl.BlockSpec((tm, wd), lambda i: (i, 0))
    out_specs = [row_spec(width)]
    out_shape = [jax.ShapeDtypeStruct((rows, width), BF16)]
    if want_zv:
        out_specs.append(row_spec(width))
        out_shape.append(jax.ShapeDtypeStruct((rows, width), F32))
    est = (2 * tm * d * 4 + w_in.size * 2 + ws_mat.size * 2 + bias_full.size * 4
           + 2 * tm * width * 2 + (2 * tm * width * 4 if want_zv else 0)
           + tm * width * 2 + 3 * tm * 2 * width * 4)
    res = pl.pallas_call(
        functools.partial(_sgu_body, groups, want_zv),
        grid=(rows // tm,),
        in_specs=[row_spec(d), _resident((1, d)), _resident(w_in.shape), _resident((1, width)),
                  _resident(ws_mat.shape), _resident(bias_full.shape)],
        out_specs=out_specs,
        out_shape=out_shape,
        scratch_shapes=[pltpu.VMEM((tm, width), BF16)],
        compiler_params=pltpu.CompilerParams(
            dimension_semantics=("arbitrary",), vmem_limit_bytes=_vmem_limit(est)),
        name="spatial_gating",
    )(x, g.reshape(1, d), w_in, norm_g.reshape(1, width), ws_mat, bias_full)
    return res if want_zv else (res[0], None)


def _sgu_spatial(w_s, b_s, seq_len):
    groups = w_s.shape[0]
    span = min(seq_len, SGU_CHUNK)
    idx = jnp.arange(span) // SGU_STREAM_CHUNK
    w = jnp.where((idx[None, :] <= idx[:, None])[None], w_s[:, :span, :span], 0.0)
    reps = SGU_CHUNK // span
    if reps > 1:
        w = jnp.einsum('ab,gts->gatbs', jnp.eye(reps, dtype=w.dtype), w).reshape(groups, SGU_CHUNK, SGU_CHUNK)
    bias = jnp.tile(b_s[:, :span], (1, reps))
    return w.astype(BF16), bias


def kernel(x_prompt, x_sample, cache_k, cache_v, cache_logf, state_pool, norm_g, ffn_w_in, ffn_w_down,
           even_w_in, even_b_f, pool_w, pool_scale, even_w_out, sgu_w_in, sgu_norm_g, sgu_w_s, sgu_b_s,
           sgu_w_out, final_g):
    b, s, d = x_prompt.shape
    bs, t, _ = x_sample.shape
    depth = norm_g.shape[0]
    heads = even_b_f.shape[1]
    width = heads * FOX_HEAD_DIM
    pool_width = pool_scale.shape[1]
    n_groups = pool_w.shape[1]
    past = cache_k.shape[2]
    sgu_width = sgu_norm_g.shape[1]
    sgu_groups = sgu_w_s.shape[1]
    sgu_gdim = sgu_width // sgu_groups

    xp = x_prompt.reshape(b * s, d)
    xs = x_sample.reshape(bs * t, d)
    kp_l, vp_l, fp_l, up_l = [], [], [], []
    ks_l, vs_l, fs_l, us_l, zs_l = [], [], [], [], []

    w_in_all = ffn_w_in.astype(BF16)
    w_down_all = ffn_w_down.astype(BF16)
    for l in range(depth):
        last = final_g if l == depth - 1 else None
        ffn_b = functools.partial(_ffn, g=norm_g[l, 2], w_in_all=w_in_all, w_down_all=w_down_all,
                                  which=(l, 1), final_g=last)
        xp = _ffn(xp, [], norm_g[l, 0], w_in_all, w_down_all, (l, 0))
        xs = _ffn(xs, [], norm_g[l, 0], w_in_all, w_down_all, (l, 0))
        if l % 2 == 0:
            e = l // 2
            w_e = even_w_in[e]
            f_cols = _replicate_gates(w_e[:, 3 * width:3 * width + heads], heads)
            w_all = jnp.concatenate([w_e[:, :3 * width], w_e[:, 3 * width + heads:], f_cols], axis=1).astype(BF16)
            bf_pad = _replicate_gates(even_b_f[e], heads).reshape(1, V7X_LANES)
            pw = pool_w[e].astype(BF16)
            psc = pool_scale[e].reshape(n_groups, 1, pool_width // n_groups)
            w_out_att = even_w_out[e, :width].astype(BF16)
            w_out_pool = even_w_out[e, width:].astype(BF16)

            q, kb, vb, k, v, u, lf, lfw = _even_proj(xp, norm_g[l, 1], w_all, bf_pad, width, heads)
            hist = jnp.zeros((b, POOL_HIST + 1, pool_width), F32)
            att, pool = _attention_pool(heads, q.reshape(b, s, width), kb.reshape(b, s, width),
                                        vb.reshape(b, s, width), lfw.reshape(b, s, V7X_LANES),
                                        u.reshape(b, s, pool_width), hist, pw, psc)
            xp = ffn_b(xp, [(att.reshape(b * s, width), w_out_att), (pool.reshape(b * s, pool_width), w_out_pool)])
            kp_l.append(k.reshape(b, s, heads, FOX_HEAD_DIM))
            vp_l.append(v.reshape(b, s, heads, FOX_HEAD_DIM))
            fp_l.append(jnp.transpose(lf.reshape(heads, b, s), (1, 2, 0)))
            up_l.append(u.reshape(b, s, pool_width)[:, s - POOL_HIST:])

            q, kb, vb, k, v, u, lf, lfw = _even_proj(xs, norm_g[l, 1], w_all, bf_pad, width, heads)
            hist = jnp.pad(state_pool[e], ((0, 0), (1, 0), (0, 0)))
            channel_major = lambda c: jnp.transpose(c, (0, 2, 3, 1)).reshape(bs, width, past)
            past_kvf = (channel_major(cache_k[e]), channel_major(cache_v[e]),
                        _replicate_gates(jnp.transpose(cache_logf[e], (0, 2, 1)), heads, axis=1))
            att, pool = _attention_pool(heads, q.reshape(bs, t, width), kb.reshape(bs, t, width),
                                        vb.reshape(bs, t, width), lfw.reshape(bs, t, V7X_LANES),
                                        u.reshape(bs, t, pool_width), hist, pw, psc, past_kvf)
            xs = ffn_b(xs, [(att.reshape(bs * t, width), w_out_att), (pool.reshape(bs * t, pool_width), w_out_pool)])
            ks_l.append(k.reshape(bs, t, heads, FOX_HEAD_DIM))
            vs_l.append(v.reshape(bs, t, heads, FOX_HEAD_DIM))
            fs_l.append(jnp.transpose(lf.reshape(heads, bs, t), (1, 2, 0)))
            u_ext = jnp.concatenate([state_pool[e], u.reshape(bs, t, pool_width)], axis=1)
            us_l.append(u_ext[:, -POOL_HIST:])
        else:
            o = l // 2
            w_in = sgu_w_in[o].astype(BF16)
            w_out = sgu_w_out[o].astype(BF16)
            ws_p, bias_p = _sgu_spatial(sgu_w_s[o], sgu_b_s[o], s)
            ws_s, bias_s = _sgu_spatial(sgu_w_s[o], sgu_b_s[o], t)
            expand = lambda bias: jnp.repeat(bias.T, sgu_gdim, axis=1)
            gated, _ = _sgu(xp, norm_g[l, 1], w_in, sgu_norm_g[o], ws_p, expand(bias_p), False)
            xp = ffn_b(xp, [(gated, w_out)])
            gated, zv = _sgu(xs, norm_g[l, 1], w_in, sgu_norm_g[o], ws_s, expand(bias_s), True)
            xs = ffn_b(xs, [(gated, w_out)])
            zs_l.append(zv.reshape(bs, t, sgu_width))

    return (xp.reshape(b, s, d), xs.reshape(bs, t, d),
            jnp.stack(kp_l), jnp.stack(vp_l), jnp.stack(fp_l), jnp.stack(up_l),
            jnp.stack(ks_l), jnp.stack(vs_l), jnp.stack(fs_l), jnp.stack(us_l), jnp.stack(zs_l))
```

```python
import functools
import math

import jax
import jax.numpy as jnp
from jax import lax
from jax.experimental import pallas as pl
from jax.experimental.pallas import tpu as pltpu

F32 = jnp.float32
BF16 = jnp.bfloat16

EPS = 1e-6
FOX_HEAD_DIM = 64
POOL_WINDOWS = (2, 4, 8, 16)
POOL_HIST = max(POOL_WINDOWS) - 1
SGU_STREAM_CHUNK = 64
SGU_CHUNK = 128

V7X_LANES = 128
V7X_MXU_DIM = 256
V7X_VMEM_BYTES = 64 * 1024 * 1024
VMEM_COMPILER_RESERVE = 8 * 1024 * 1024

ROW_TILE = 1024
SUB_TILE = 512
FF_CHUNK = V7X_MXU_DIM
ATT_Q_BLOCK = 256
ATT_KV_CHUNK = V7X_MXU_DIM
AUG_LANES = V7X_LANES
N_SPLIT = 3
GATE_COPIES = 2 * N_SPLIT
LOG2E = math.log2(math.e)
GELU_A = math.sqrt(2.0 / math.pi)
GELU_B = 0.044715


def _vmem_limit(est_bytes):
    return int(min(max(est_bytes, 16 * 1024 * 1024), V7X_VMEM_BYTES - VMEM_COMPILER_RESERVE))


def _resident(shape):
    nd = len(shape)
    return pl.BlockSpec(shape, lambda *_: (0,) * nd, pipeline_mode=pl.Buffered(1))


def _rms(x, g):
    ms = jnp.mean(x * x, axis=-1, keepdims=True)
    return x * lax.rsqrt(ms + EPS) * g


def _round_up(n, m):
    return (n + m - 1) // m * m


def _ffn_body(n_mix, n_chunks, has_final, *refs):
    x_ref = refs[0]
    a_refs = refs[1:1 + n_mix]
    w_refs = refs[1 + n_mix:1 + 2 * n_mix]
    g_ref, win_ref, wdown_ref = refs[1 + 2 * n_mix:4 + 2 * n_mix]
    pos = 4 + 2 * n_mix
    fg_ref = refs[pos] if has_final else None
    pos += int(has_final)
    o_ref, h_scr, act_scr = refs[pos:pos + 3]

    d_ff = n_chunks * FF_CHUNK
    for r0 in range(0, x_ref.shape[0], SUB_TILE):
        rows = slice(r0, r0 + SUB_TILE)
        x = x_ref[rows, :]
        for a_ref, w_ref in zip(a_refs, w_refs):
            x = x + jnp.dot(a_ref[rows, :], w_ref[...], preferred_element_type=F32)
        o_ref[rows, :] = x
        h_scr[rows, :] = _rms(x, g_ref[...]).astype(BF16)
        for j in range(n_chunks):
            cols = slice(j * FF_CHUNK, (j + 1) * FF_CHUNK)
            up_cols = slice(d_ff + j * FF_CHUNK, d_ff + (j + 1) * FF_CHUNK)
            gate = jnp.dot(h_scr[rows, :], win_ref[:, cols], preferred_element_type=F32)
            up = jnp.dot(h_scr[rows, :], win_ref[:, up_cols], preferred_element_type=F32)
            act = gate * (1.0 / (1.0 + jnp.exp(-gate))) * up
            act_scr[rows, cols] = act.astype(BF16)
        y = jnp.dot(act_scr[rows, :], wdown_ref[...], preferred_element_type=F32)
        out = o_ref[rows, :] + 0.5 * y
        if has_final:
            out = _rms(out, fg_ref[...])
        o_ref[rows, :] = out


def _ffn(x, mixes, g, w_in_all, w_down_all, which, final_g=None):
    rows, d = x.shape
    d_ff = w_down_all.shape[2]
    assert w_in_all.shape[2:] == (d, 2 * d_ff) and d_ff % FF_CHUNK == 0
    n_chunks = d_ff // FF_CHUNK
    picked = lambda shape: pl.BlockSpec((None, None) + shape, lambda i: which + (0, 0),
                                        pipeline_mode=pl.Buffered(1))
    two_chunk = 2 * FF_CHUNK
    tm = ROW_TILE
    n_mix = len(mixes)
    has_final = final_g is not None

    row_spec = lambda width: pl.BlockSpec((tm, width), lambda i: (i, 0))
    in_specs = [row_spec(d)]
    in_specs += [row_spec(a.shape[1]) for a, _ in mixes]
    in_specs += [_resident(w.shape) for _, w in mixes]
    in_specs += [_resident((1, d)), picked((d, 2 * d_ff)), picked((d_ff, d))]
    args = [x] + [a for a, _ in mixes] + [w for _, w in mixes] + [g.reshape(1, d), w_in_all, w_down_all]
    if has_final:
        in_specs.append(_resident((1, d)))
        args.append(final_g.reshape(1, d))

    est = (4 * tm * d * 4
           + sum(2 * tm * a.shape[1] * 2 + w.size * 2 for a, w in mixes)
           + 3 * d * d_ff * 2
           + tm * d * 2 + tm * d_ff * 2
           + SUB_TILE * two_chunk * 4 * 2 + SUB_TILE * d * 4 * 2)
    return pl.pallas_call(
        functools.partial(_ffn_body, n_mix, n_chunks, has_final),
        grid=(rows // tm,),
        in_specs=in_specs,
        out_specs=row_spec(d),
        out_shape=jax.ShapeDtypeStruct((rows, d), F32),
        scratch_shapes=[pltpu.VMEM((tm, d), BF16), pltpu.VMEM((tm, d_ff), BF16)],
        compiler_params=pltpu.CompilerParams(
            dimension_semantics=("arbitrary",), vmem_limit_bytes=_vmem_limit(est)),
        name="ffn_half_step",
    )(*args)


def _even_proj_body(width, heads, x_ref, g_ref, w_ref, bf_ref,
                    q_ref, kb_ref, vb_ref, k4_ref, v4_ref, u_ref, lf_ref, lfw_ref):
    starts = range(0, x_ref.shape[0], SUB_TILE)
    zs = [jnp.dot(_rms(x_ref[r0:r0 + SUB_TILE, :], g_ref[...]).astype(BF16), w_ref[...],
                  preferred_element_type=F32) for r0 in starts]
    for r0, z in zip(starts, zs):
        rows = slice(r0, r0 + SUB_TILE)
        q_ref[rows, :] = (z[:, :width] * (FOX_HEAD_DIM ** -0.5 * LOG2E)).astype(BF16)
        kb_ref[rows, :] = z[:, width:2 * width].astype(BF16)
        vb_ref[rows, :] = z[:, 2 * width:3 * width].astype(BF16)
        for hd in range(heads):
            head_rows = pl.ds(r0 * heads + hd, SUB_TILE, stride=heads)
            k4_ref[head_rows, :] = z[:, width + hd * FOX_HEAD_DIM:width + (hd + 1) * FOX_HEAD_DIM]
            v4_ref[head_rows, :] = z[:, 2 * width + hd * FOX_HEAD_DIM:2 * width + (hd + 1) * FOX_HEAD_DIM]
        u_ref[rows, :] = z[:, 3 * width:4 * width]
        f = z[:, 4 * width:4 * width + V7X_LANES] + bf_ref[...]
        logf = jnp.minimum(f, 0.0) - jnp.log(1.0 + jnp.exp(-jnp.abs(f)))
        lf_ref[:, rows] = logf.T[:heads, :]
        lfw_ref[rows, :] = logf


def _replicate_gates(cols, heads, axis=-1):
    axis = axis % cols.ndim
    src = jnp.array([h for h in range(heads) for _ in range(GATE_COPIES)])
    pad_shape = cols.shape[:axis] + (V7X_LANES - heads * (1 + GATE_COPIES),) + cols.shape[axis + 1:]
    return jnp.concatenate([cols, jnp.take(cols, src, axis=axis), jnp.zeros(pad_shape, cols.dtype)], axis=axis)


def _even_proj(x, g, w_all, bf_pad, width, heads):
    rows, d = x.shape
    tm = ROW_TILE
    n_out = w_all.shape[1]
    row_spec = lambda wd: pl.BlockSpec((tm, wd), lambda i: (i, 0))
    head_rows = pl.BlockSpec((tm * heads, FOX_HEAD_DIM), lambda i: (i, 0))
    est = (2 * tm * d * 4 + w_all.size * 2 + 2 * tm * width * (3 * 2 + 4)
           + 2 * 2 * tm * heads * V7X_LANES * 4
           + 4 * tm * V7X_LANES * 4 + tm * n_out * 4 * 2 + tm * d * 2)
    return pl.pallas_call(
        functools.partial(_even_proj_body, width, heads),
        grid=(rows // tm,),
        in_specs=[row_spec(d), _resident((1, d)), _resident(w_all.shape), _resident((1, V7X_LANES))],
        out_specs=[row_spec(width), row_spec(width), row_spec(width), head_rows, head_rows,
                   row_spec(width), pl.BlockSpec((heads, tm), lambda i: (0, i)), row_spec(V7X_LANES)],
        out_shape=[jax.ShapeDtypeStruct((rows, width), BF16),
                   jax.ShapeDtypeStruct((rows, width), BF16),
                   jax.ShapeDtypeStruct((rows, width), BF16),
                   jax.ShapeDtypeStruct((rows * heads, FOX_HEAD_DIM), F32),
                   jax.ShapeDtypeStruct((rows * heads, FOX_HEAD_DIM), F32),
                   jax.ShapeDtypeStruct((rows, width), F32),
                   jax.ShapeDtypeStruct((heads, rows), F32),
                   jax.ShapeDtypeStruct((rows, V7X_LANES), F32)],
        compiler_params=pltpu.CompilerParams(
            dimension_semantics=("arbitrary",), vmem_limit_bytes=_vmem_limit(est)),
        name="even_projection",
    )(x, g.reshape(1, d), w_all, bf_pad)


def _attn_body(past, t_new, tq, heads, *refs):
    n_all = past + t_new
    n_pad = _round_up(n_all, V7X_LANES)
    if past:
        (q_ref, k_ref, v_ref, lf_ref, kp_ref, vp_ref, lfp_ref, u_ref, hist_ref, pw_ref, ps_ref,
         att_ref, pool_ref, ca, cb, c_hi, c_mid, c_lo, kcat, qcat, vcat, pa, pb) = refs
    else:
        (q_ref, k_ref, v_ref, lf_ref, u_ref, hist_ref, pw_ref, ps_ref,
         att_ref, pool_ref, ca, cb, c_hi, c_mid, c_lo, kcat, qcat, vcat, pa, pb) = refs
    pair = pl.program_id(1)

    @pl.when(pair == 0)
    def _():
        pad = 8
        ca[0:pad, :] = jnp.zeros((pad, V7X_LANES), F32)
        cb[0:pad, :] = jnp.zeros((pad, V7X_LANES), F32)
        if past:
            ca[pad:pad + past, :] = lfp_ref[...].T
        ca[pad + past:pad + n_all, :] = lf_ref[...]
        src, dst = ca, cb
        s = 1
        while s < n_all:
            if s < pad:
                dst[pad:pad + n_all, :] = src[pad:pad + n_all, :] + src[pad - s:pad + n_all - s, :]
            else:
                dst[pad:pad + s, :] = src[pad:pad + s, :]
                dst[pad + s:pad + n_all, :] = src[pad + s:pad + n_all, :] + src[pad:pad + n_all - s, :]
            src, dst = dst, src
            s *= 2
        c = src[pad:pad + n_all, :] * LOG2E
        hi = c.astype(BF16)
        r1 = c - hi.astype(F32)
        mid = r1.astype(BF16)
        c_hi[...] = hi
        c_mid[...] = mid
        c_lo[...] = (r1 - mid.astype(F32)).astype(BF16)
        vcat[0:n_all, V7X_LANES:2 * V7X_LANES] = jnp.ones((n_all, V7X_LANES), BF16)
        if n_pad > n_all:
            kcat[n_all:n_pad, :] = jnp.zeros((n_pad - n_all, V7X_LANES + AUG_LANES), BF16)
            vcat[n_all:n_pad, :] = jnp.zeros((n_pad - n_all, 2 * V7X_LANES), BF16)

    ln = lax.broadcasted_iota(jnp.int32, (1, AUG_LANES), 1)
    base = heads + 2 * GATE_COPIES * pair

    def pat(lanes, value=1.0):
        out = jnp.zeros((1, AUG_LANES), F32)
        for l in lanes:
            out = jnp.where(ln == base + l, value, out)
        return out.astype(BF16)

    k_lanes = lambda j: [hh * GATE_COPIES + N_SPLIT + j for hh in range(2)]
    aug_k = (c_hi[...] * pat(k_lanes(0), -1.0) + c_mid[...] * pat(k_lanes(1), -1.0)
             + c_lo[...] * pat(k_lanes(2), -1.0)
             + pat([hh * GATE_COPIES + j for hh in range(2) for j in range(N_SPLIT)]))
    if past:
        kcat[0:past, 0:V7X_LANES] = kp_ref[...].T.astype(BF16)
        vcat[0:past, 0:V7X_LANES] = vp_ref[...].T.astype(BF16)
    kcat[past:n_all, 0:V7X_LANES] = k_ref[...]
    vcat[past:n_all, 0:V7X_LANES] = v_ref[...]
    kcat[0:n_all, V7X_LANES:V7X_LANES + AUG_LANES] = aug_k

    qv = q_ref[...]
    hi_q, mid_q, lo_q = c_hi[past:n_all, :], c_mid[past:n_all, :], c_lo[past:n_all, :]
    for hh in range(2):
        lo_lane = hh * GATE_COPIES
        aug_q = (hi_q * pat([lo_lane]) + mid_q * pat([lo_lane + 1]) + lo_q * pat([lo_lane + 2])
                 + pat(range(lo_lane + N_SPLIT, lo_lane + GATE_COPIES)))
        in_head = (ln >= hh * FOX_HEAD_DIM) & (ln < (hh + 1) * FOX_HEAD_DIM)
        q_h = qv * jnp.where(in_head, 1.0, 0.0).astype(BF16)
        for i in range(t_new // tq):
            qcat[i, hh * tq:(hh + 1) * tq, 0:V7X_LANES] = q_h[i * tq:(i + 1) * tq]
            qcat[i, hh * tq:(hh + 1) * tq, V7X_LANES:V7X_LANES + AUG_LANES] = aug_q[i * tq:(i + 1) * tq]

    def pool_group():
        lead = 2 * (POOL_HIST + 1)
        half = POOL_HIST + 1
        pa[0:half, :] = jnp.zeros((half, V7X_LANES), F32)
        pb[0:half, :] = jnp.zeros((half, V7X_LANES), F32)
        pa[half:lead, :] = hist_ref[...]
        pa[lead:lead + t_new, :] = u_ref[...]
        src, dst = pa, pb
        win = None
        shift = 1
        for gi in range(len(POOL_WINDOWS)):
            dst[half:lead + t_new, :] = src[half:lead + t_new, :] + src[half - shift:lead + t_new - shift, :]
            cur = dst[lead:lead + t_new, :]
            win = cur if win is None else jnp.where(pair >= gi, cur, win)
            src, dst = dst, src
            shift *= 2
        width = jnp.left_shift(2, pair)
        posn = lax.broadcasted_iota(jnp.int32, (t_new, V7X_LANES), 0) + past
        cnt = jnp.minimum(width, posn + 1).astype(F32)
        dlt = (win / cnt - u_ref[...]).astype(BF16)
        y = jnp.dot(dlt, pw_ref[...], preferred_element_type=F32)
        pool_ref[...] = (y * ps_ref[...]).astype(BF16)

    n_blocks = t_new // tq
    nt = (((1,), (1,)), ((), ()))
    lane_o = lax.broadcasted_iota(jnp.int32, (tq, V7X_LANES), 1)

    def scores(i):
        r0 = i * tq
        kend = _round_up(past + r0 + tq, V7X_LANES)
        d0 = (past + r0) // V7X_LANES * V7X_LANES
        qs = qcat[i]
        bounds = [(c0, min(c0 + ATT_KV_CHUNK, kend)) for c0 in range(0, kend, ATT_KV_CHUNK)]
        s_list = []
        mx = None
        for c0, c1 in bounds:
            s = lax.dot_general(qs, kcat[c0:c1, :], nt, preferred_element_type=F32)
            if c1 > d0:
                qpos = jnp.bitwise_and(lax.broadcasted_iota(jnp.int32, (2 * tq, c1 - c0), 0), tq - 1) + (past + r0)
                kpos = lax.broadcasted_iota(jnp.int32, (2 * tq, c1 - c0), 1) + c0
                s = jnp.where(kpos <= qpos, s, -jnp.inf)
            s_list.append(s)
            fold = s[:, 0:V7X_LANES]
            for l0 in range(V7X_LANES, c1 - c0, V7X_LANES):
                fold = jnp.maximum(fold, s[:, l0:l0 + V7X_LANES])
            mx = fold if mx is None else jnp.maximum(mx, fold)
        return s_list, bounds, jnp.max(mx, axis=1, keepdims=True)

    def finish(i, s_list, bounds, m):
        o = None
        for s, (c0, c1) in zip(s_list, bounds):
            pv = jnp.dot(jnp.exp2(s - m).astype(BF16), vcat[c0:c1, :], preferred_element_type=F32)
            o = pv if o is None else o + pv
        o = o[:, 0:V7X_LANES] / o[:, V7X_LANES:2 * V7X_LANES]
        att_ref[i * tq:(i + 1) * tq, :] = jnp.where(lane_o < FOX_HEAD_DIM, o[0:tq], o[tq:2 * tq]).astype(BF16)

    ahead = scores(0)
    for i in range(n_blocks):
        current, ahead = ahead, (scores(i + 1) if i + 1 < n_blocks else None)
        finish(i, *current)
    pool_group()


def _attention_pool(heads, q, k, v, logf_wide, u, hist16, pool_w, pool_scale, past_kvf=None):
    b, t_new, width = q.shape
    n_pairs = width // V7X_LANES
    assert heads + heads * GATE_COPIES <= AUG_LANES
    past = 0 if past_kvf is None else past_kvf[0].shape[2]
    n_all = past + t_new
    n_pad = _round_up(n_all, V7X_LANES)
    tq = min(ATT_Q_BLOCK, t_new)
    assert tq & (tq - 1) == 0 and t_new % tq == 0 and heads % 2 == 0

    slab = lambda rows: pl.BlockSpec((None, rows, V7X_LANES), lambda bi, pi: (bi, 0, pi))
    gates = lambda rows: pl.BlockSpec((None, rows, V7X_LANES), lambda bi, pi: (bi, 0, 0))
    in_specs = [slab(t_new), slab(t_new), slab(t_new), gates(t_new)]
    args = [q, k, v, logf_wide]
    if past:
        channel_major = pl.BlockSpec((None, V7X_LANES, past), lambda bi, pi: (bi, pi, 0))
        in_specs += [channel_major, channel_major,
                     pl.BlockSpec((None, V7X_LANES, past), lambda bi, pi: (bi, 0, 0))]
        args += list(past_kvf)
    in_specs += [slab(t_new), slab(POOL_HIST + 1),
                 pl.BlockSpec((None, V7X_LANES, V7X_LANES), lambda bi, pi: (pi, 0, 0)),
                 pl.BlockSpec((None, 1, V7X_LANES), lambda bi, pi: (pi, 0, 0))]
    args += [u, hist16, pool_w, pool_scale]

    lane_pad_f32 = V7X_LANES * 4
    scratch = [
        pltpu.VMEM((8 + n_all, V7X_LANES), F32), pltpu.VMEM((8 + n_all, V7X_LANES), F32),
        pltpu.VMEM((n_all, AUG_LANES), BF16), pltpu.VMEM((n_all, AUG_LANES), BF16),
        pltpu.VMEM((n_all, AUG_LANES), BF16),
        pltpu.VMEM((n_pad, V7X_LANES + AUG_LANES), BF16),
        pltpu.VMEM((t_new // tq, 2 * tq, V7X_LANES + AUG_LANES), BF16),
        pltpu.VMEM((n_pad, 2 * V7X_LANES), BF16),
        pltpu.VMEM((2 * (POOL_HIST + 1) + t_new, V7X_LANES), F32),
        pltpu.VMEM((2 * (POOL_HIST + 1) + t_new, V7X_LANES), F32),
    ]
    est = (2 * t_new * V7X_LANES * (2 + 2 + 2 + 4 + 2 + 2) + 2 * t_new * lane_pad_f32
           + 2 * past * (2 * V7X_LANES * 4 + lane_pad_f32)
           + 3 * (8 + n_all) * lane_pad_f32 + n_all * lane_pad_f32
           + 2 * n_pad * (V7X_LANES + AUG_LANES) * 2 + 2 * t_new * (V7X_LANES + AUG_LANES) * 2
           + 2 * (32 + t_new) * lane_pad_f32
           + 4 * 2 * tq * n_pad * 4 + 10 * n_all * lane_pad_f32)
    return pl.pallas_call(
        functools.partial(_attn_body, past, t_new, tq, heads),
        grid=(b, n_pairs),
        in_specs=in_specs,
        out_specs=[slab(t_new), slab(t_new)],
        out_shape=[jax.ShapeDtypeStruct((b, t_new, width), BF16),
                   jax.ShapeDtypeStruct((b, t_new, width), BF16)],
        scratch_shapes=scratch,
        compiler_params=pltpu.CompilerParams(
            dimension_semantics=("arbitrary", "arbitrary"), vmem_limit_bytes=_vmem_limit(est)),
        name="fox_attention_pool",
    )(*args)


def _sgu_body(groups, want_zv, x_ref, g_ref, w_ref, ng_ref, ws_ref, bias_ref, *rest):
    if want_zv:
        o_ref, zv_ref, zvb_scr = rest
    else:
        o_ref, zvb_scr = rest
        zv_ref = None
    width = ng_ref.shape[1]
    gdim = width // groups
    starts = range(0, x_ref.shape[0], SUB_TILE)
    zs = [jnp.dot(_rms(x_ref[r0:r0 + SUB_TILE, :], g_ref[...]).astype(BF16), w_ref[...],
                  preferred_element_type=F32) for r0 in starts]
    for r0, z in zip(starts, zs):
        rows = slice(r0, r0 + SUB_TILE)
        half_z = 0.5 * z
        z = half_z + half_z * jnp.tanh(z * (GELU_A + (GELU_A * GELU_B) * (z * z)))
        zv = _rms(z[:, width:], ng_ref[...])
        if want_zv:
            zv_ref[rows, :] = zv
        zvb_scr[rows, :] = zv.astype(BF16)
        for ch in range(SUB_TILE // SGU_CHUNK):
            rs = slice(ch * SGU_CHUNK, (ch + 1) * SGU_CHUNK)
            out_rs = slice(r0 + ch * SGU_CHUNK, r0 + (ch + 1) * SGU_CHUNK)
            for gi in range(groups):
                cs = slice(gi * gdim, (gi + 1) * gdim)
                mix = jnp.dot(ws_ref[gi], zvb_scr[out_rs, cs], preferred_element_type=F32) + bias_ref[:, cs]
                o_ref[out_rs, cs] = (z[rs, cs] * mix).astype(BF16)


def _sgu(x, g, w_in, norm_g, ws_mat, bias_full, want_zv):
    rows, d = x.shape
    width = norm_g.shape[0]
    groups = ws_mat.shape[0]
    tm = ROW_TILE
    row_spec = lambda wd: pl.BlockSpec((tm, wd), lambda i: (i, 0))
    out_specs = [row_spec(width)]
    out_shape = [jax.ShapeDtypeStruct((rows, width), BF16)]
    if want_zv:
        out_specs.append(row_spec(width))
        out_shape.append(jax.ShapeDtypeStruct((rows, width), F32))
    est = (2 * tm * d * 4 + w_in.size * 2 + ws_mat.size * 2 + bias_full.size * 4
           + 2 * tm * width * 2 + (2 * tm * width * 4 if want_zv else 0)
           + tm * width * 2 + 3 * tm * 2 * width * 4)
    res = pl.pallas_call(
        functools.partial(_sgu_body, groups, want_zv),
        grid=(rows // tm,),
        in_specs=[row_spec(d), _resident((1, d)), _resident(w_in.shape), _resident((1, width)),
                  _resident(ws_mat.shape), _resident(bias_full.shape)],
        out_specs=out_specs,
        out_shape=out_shape,
        scratch_shapes=[pltpu.VMEM((tm, width), BF16)],
        compiler_params=pltpu.CompilerParams(
            dimension_semantics=("arbitrary",), vmem_limit_bytes=_vmem_limit(est)),
        name="spatial_gating",
    )(x, g.reshape(1, d), w_in, norm_g.reshape(1, width), ws_mat, bias_full)
    return res if want_zv else (res[0], None)


def _sgu_spatial(w_s, b_s, seq_len):
    groups = w_s.shape[0]
    span = min(seq_len, SGU_CHUNK)
    idx = jnp.arange(span) // SGU_STREAM_CHUNK
    w = jnp.where((idx[None, :] <= idx[:, None])[None], w_s[:, :span, :span], 0.0)
    reps = SGU_CHUNK // span
    if reps > 1:
        w = jnp.einsum('ab,gts->gatbs', jnp.eye(reps, dtype=w.dtype), w).reshape(groups, SGU_CHUNK, SGU_CHUNK)
    bias = jnp.tile(b_s[:, :span], (1, reps))
    return w.astype(BF16), bias


def kernel(x_prompt, x_sample, cache_k, cache_v, cache_logf, state_pool, norm_g, ffn_w_in, ffn_w_down,
           even_w_in, even_b_f, pool_w, pool_scale, even_w_out, sgu_w_in, sgu_norm_g, sgu_w_s, sgu_b_s,
           sgu_w_out, final_g):
    b, s, d = x_prompt.shape
    bs, t, _ = x_sample.shape
    depth = norm_g.shape[0]
    heads = even_b_f.shape[1]
    width = heads * FOX_HEAD_DIM
    pool_width = pool_scale.shape[1]
    n_groups = pool_w.shape[1]
    past = cache_k.shape[2]
    sgu_width = sgu_norm_g.shape[1]
    sgu_groups = sgu_w_s.shape[1]
    sgu_gdim = sgu_width // sgu_groups

    xp = x_prompt.reshape(b * s, d)
    xs = x_sample.reshape(bs * t, d)
    kp_l, vp_l, fp_l, up_l = [], [], [], []
    ks_l, vs_l, fs_l, us_l, zs_l = [], [], [], [], []

    w_in_all = ffn_w_in.astype(BF16)
    w_down_all = ffn_w_down.astype(BF16)
    for l in range(depth):
        last = final_g if l == depth - 1 else None
        ffn_b = functools.partial(_ffn, g=norm_g[l, 2], w_in_all=w_in_all, w_down_all=w_down_all,
                                  which=(l, 1), final_g=last)
        xp = _ffn(xp, [], norm_g[l, 0], w_in_all, w_down_all, (l, 0))
        xs = _ffn(xs, [], norm_g[l, 0], w_in_all, w_down_all, (l, 0))
        if l % 2 == 0:
            e = l // 2
            w_e = even_w_in[e]
            f_cols = _replicate_gates(w_e[:, 3 * width:3 * width + heads], heads)
            w_all = jnp.concatenate([w_e[:, :3 * width], w_e[:, 3 * width + heads:], f_cols], axis=1).astype(BF16)
            bf_pad = _replicate_gates(even_b_f[e], heads).reshape(1, V7X_LANES)
            pw = pool_w[e].astype(BF16)
            psc = pool_scale[e].reshape(n_groups, 1, pool_width // n_groups)
            w_out_att = even_w_out[e, :width].astype(BF16)
            w_out_pool = even_w_out[e, width:].astype(BF16)

            q, kb, vb, k, v, u, lf, lfw = _even_proj(xp, norm_g[l, 1], w_all, bf_pad, width, heads)
            hist = jnp.zeros((b, POOL_HIST + 1, pool_width), F32)
            att, pool = _attention_pool(heads, q.reshape(b, s, width), kb.reshape(b, s, width),
                                        vb.reshape(b, s, width), lfw.reshape(b, s, V7X_LANES),
                                        u.reshape(b, s, pool_width), hist, pw, psc)
            xp = ffn_b(xp, [(att.reshape(b * s, width), w_out_att), (pool.reshape(b * s, pool_width), w_out_pool)])
            kp_l.append(k.reshape(b, s, heads, FOX_HEAD_DIM))
            vp_l.append(v.reshape(b, s, heads, FOX_HEAD_DIM))
            fp_l.append(jnp.transpose(lf.reshape(heads, b, s), (1, 2, 0)))
            up_l.append(u.reshape(b, s, pool_width)[:, s - POOL_HIST:])

            q, kb, vb, k, v, u, lf, lfw = _even_proj(xs, norm_g[l, 1], w_all, bf_pad, width, heads)
            hist = jnp.pad(state_pool[e], ((0, 0), (1, 0), (0, 0)))
            channel_major = lambda c: jnp.transpose(c, (0, 2, 3, 1)).reshape(bs, width, past)
            past_kvf = (channel_major(cache_k[e]), channel_major(cache_v[e]),
                        _replicate_gates(jnp.transpose(cache_logf[e], (0, 2, 1)), heads, axis=1))
            att, pool = _attention_pool(heads, q.reshape(bs, t, width), kb.reshape(bs, t, width),
                                        vb.reshape(bs, t, width), lfw.reshape(bs, t, V7X_LANES),
                                        u.reshape(bs, t, pool_width), hist, pw, psc, past_kvf)
            xs = ffn_b(xs, [(att.reshape(bs * t, width), w_out_att), (pool.reshape(bs * t, pool_width), w_out_pool)])
            ks_l.append(k.reshape(bs, t, heads, FOX_HEAD_DIM))
            vs_l.append(v.reshape(bs, t, heads, FOX_HEAD_DIM))
            fs_l.append(jnp.transpose(lf.reshape(heads, bs, t), (1, 2, 0)))
            u_ext = jnp.concatenate([state_pool[e], u.reshape(bs, t, pool_width)], axis=1)
            us_l.append(u_ext[:, -POOL_HIST:])
        else:
            o = l // 2
            w_in = sgu_w_in[o].astype(BF16)
            w_out = sgu_w_out[o].astype(BF16)
            ws_p, bias_p = _sgu_spatial(sgu_w_s[o], sgu_b_s[o], s)
            ws_s, bias_s = _sgu_spatial(sgu_w_s[o], sgu_b_s[o], t)
            expand = lambda bias: jnp.repeat(bias.T, sgu_gdim, axis=1)
            gated, _ = _sgu(xp, norm_g[l, 1], w_in, sgu_norm_g[o], ws_p, expand(bias_p), False)
            xp = ffn_b(xp, [(gated, w_out)])
            gated, zv = _sgu(xs, norm_g[l, 1], w_in, sgu_norm_g[o], ws_s, expand(bias_s), True)
            xs = ffn_b(xs, [(gated, w_out)])
            zs_l.append(zv.reshape(bs, t, sgu_width))

    return (xp.reshape(b, s, d), xs.reshape(bs, t, d),
            jnp.stack(kp_l), jnp.stack(vp_l), jnp.stack(fp_l), jnp.stack(up_l),
            jnp.stack(ks_l), jnp.stack(vs_l), jnp.stack(fs_l), jnp.stack(us_l), jnp.stack(zs_l))
```

```python
import functools
import math

import jax
import jax.numpy as jnp
from jax import lax
from jax.experimental import pallas as pl
from jax.experimental.pallas import tpu as pltpu

F32 = jnp.float32
BF16 = jnp.bfloat16

EPS = 1e-6
FOX_HEAD_DIM = 64
POOL_WINDOWS = (2, 4, 8, 16)
POOL_HIST = max(POOL_WINDOWS) - 1
SGU_STREAM_CHUNK = 64
SGU_CHUNK = 128

V7X_LANES = 128
V7X_MXU_DIM = 256
V7X_VMEM_BYTES = 64 * 1024 * 1024
VMEM_COMPILER_RESERVE = 8 * 1024 * 1024

ROW_TILE = 1024
SUB_TILE = 512
FF_CHUNK = V7X_MXU_DIM
ATT_Q_BLOCK = 256
ATT_KV_CHUNK = V7X_MXU_DIM
ATT_LOOKAHEAD = 1
AUG_LANES = V7X_LANES
N_SPLIT = 3
GATE_COPIES = 2 * N_SPLIT
LOG2E = math.log2(math.e)
GELU_A = math.sqrt(2.0 / math.pi)
GELU_B = 0.044715


def _vmem_limit(est_bytes):
    return int(min(max(est_bytes, 16 * 1024 * 1024), V7X_VMEM_BYTES - VMEM_COMPILER_RESERVE))


def _resident(shape):
    nd = len(shape)
    return pl.BlockSpec(shape, lambda *_: (0,) * nd, pipeline_mode=pl.Buffered(1))


def _rms(x, g):
    ms = jnp.mean(x * x, axis=-1, keepdims=True)
    return x * lax.rsqrt(ms + EPS) * g


def _round_up(n, m):
    return (n + m - 1) // m * m


def _ffn_body(n_mix, n_chunks, has_final, *refs):
    x_ref = refs[0]
    a_refs = refs[1:1 + n_mix]
    w_refs = refs[1 + n_mix:1 + 2 * n_mix]
    g_ref, win_ref, wdown_ref = refs[1 + 2 * n_mix:4 + 2 * n_mix]
    pos = 4 + 2 * n_mix
    fg_ref = refs[pos] if has_final else None
    pos += int(has_final)
    o_ref, h_scr, act_scr = refs[pos:pos + 3]

    d_ff = n_chunks * FF_CHUNK
    for r0 in range(0, x_ref.shape[0], SUB_TILE):
        rows = slice(r0, r0 + SUB_TILE)
        x = x_ref[rows, :]
        for a_ref, w_ref in zip(a_refs, w_refs):
            x = x + jnp.dot(a_ref[rows, :], w_ref[...], preferred_element_type=F32)
        o_ref[rows, :] = x
        h_scr[rows, :] = _rms(x, g_ref[...]).astype(BF16)
        for j in range(n_chunks):
            cols = slice(j * FF_CHUNK, (j + 1) * FF_CHUNK)
            up_cols = slice(d_ff + j * FF_CHUNK, d_ff + (j + 1) * FF_CHUNK)
            gate = jnp.dot(h_scr[rows, :], win_ref[:, cols], preferred_element_type=F32)
            up = jnp.dot(h_scr[rows, :], win_ref[:, up_cols], preferred_element_type=F32)
            act = gate * (1.0 / (1.0 + jnp.exp(-gate))) * up
            act_scr[rows, cols] = act.astype(BF16)
        y = jnp.dot(act_scr[rows, :], wdown_ref[...], preferred_element_type=F32)
        out = o_ref[rows, :] + 0.5 * y
        if has_final:
            out = _rms(out, fg_ref[...])
        o_ref[rows, :] = out


def _ffn(x, mixes, g, w_in_all, w_down_all, which, final_g=None):
    rows, d = x.shape
    d_ff = w_down_all.shape[2]
    assert w_in_all.shape[2:] == (d, 2 * d_ff) and d_ff % FF_CHUNK == 0
    n_chunks = d_ff // FF_CHUNK
    picked = lambda shape: pl.BlockSpec((None, None) + shape, lambda i: which + (0, 0),
                                        pipeline_mode=pl.Buffered(1))
    two_chunk = 2 * FF_CHUNK
    tm = ROW_TILE
    n_mix = len(mixes)
    has_final = final_g is not None

    row_spec = lambda width: pl.BlockSpec((tm, width), lambda i: (i, 0))
    in_specs = [row_spec(d)]
    in_specs += [row_spec(a.shape[1]) for a, _ in mixes]
    in_specs += [_resident(w.shape) for _, w in mixes]
    in_specs += [_resident((1, d)), picked((d, 2 * d_ff)), picked((d_ff, d))]
    args = [x] + [a for a, _ in mixes] + [w for _, w in mixes] + [g.reshape(1, d), w_in_all, w_down_all]
    if has_final:
        in_specs.append(_resident((1, d)))
        args.append(final_g.reshape(1, d))

    est = (4 * tm * d * 4
           + sum(2 * tm * a.shape[1] * 2 + w.size * 2 for a, w in mixes)
           + 3 * d * d_ff * 2
           + tm * d * 2 + tm * d_ff * 2
           + SUB_TILE * two_chunk * 4 * 2 + SUB_TILE * d * 4 * 2)
    return pl.pallas_call(
        functools.partial(_ffn_body, n_mix, n_chunks, has_final),
        grid=(rows // tm,),
        in_specs=in_specs,
        out_specs=row_spec(d),
        out_shape=jax.ShapeDtypeStruct((rows, d), F32),
        scratch_shapes=[pltpu.VMEM((tm, d), BF16), pltpu.VMEM((tm, d_ff), BF16)],
        compiler_params=pltpu.CompilerParams(
            dimension_semantics=("arbitrary",), vmem_limit_bytes=_vmem_limit(est)),
        name="ffn_half_step",
    )(*args)


def _even_proj_body(width, heads, channel_major, x_ref, g_ref, w_ref, bf_ref,
                    q_ref, kb_ref, vb_ref, k4_ref, v4_ref, u_ref, lf_ref, lfw_ref):
    starts = range(0, x_ref.shape[0], SUB_TILE)
    zs = [jnp.dot(_rms(x_ref[r0:r0 + SUB_TILE, :], g_ref[...]).astype(BF16), w_ref[...],
                  preferred_element_type=F32) for r0 in starts]
    for r0, z in zip(starts, zs):
        rows = slice(r0, r0 + SUB_TILE)
        q_ref[rows, :] = (z[:, :width] * (FOX_HEAD_DIM ** -0.5 * LOG2E)).astype(BF16)
        kb_ref[rows, :] = z[:, width:2 * width].astype(BF16)
        vb_ref[rows, :] = z[:, 2 * width:3 * width].astype(BF16)
        if channel_major:
            k4_ref[:, rows] = z[:, width:2 * width].T
            v4_ref[:, rows] = z[:, 2 * width:3 * width].T
        else:
            for hd in range(heads):
                head_rows = pl.ds(r0 * heads + hd, SUB_TILE, stride=heads)
                k4_ref[head_rows, :] = z[:, width + hd * FOX_HEAD_DIM:width + (hd + 1) * FOX_HEAD_DIM]
                v4_ref[head_rows, :] = z[:, 2 * width + hd * FOX_HEAD_DIM:2 * width + (hd + 1) * FOX_HEAD_DIM]
        u_ref[rows, :] = z[:, 3 * width:4 * width]
        f = z[:, 4 * width:4 * width + V7X_LANES] + bf_ref[...]
        logf = jnp.minimum(f, 0.0) - jnp.log(1.0 + jnp.exp(-jnp.abs(f)))
        lf_ref[:, rows] = logf.T[:heads, :]
        lfw_ref[rows, :] = logf


def _replicate_gates(cols, heads, axis=-1):
    axis = axis % cols.ndim
    src = jnp.array([h for h in range(heads) for _ in range(GATE_COPIES)])
    pad_shape = cols.shape[:axis] + (V7X_LANES - heads * (1 + GATE_COPIES),) + cols.shape[axis + 1:]
    return jnp.concatenate([cols, jnp.take(cols, src, axis=axis), jnp.zeros(pad_shape, cols.dtype)], axis=axis)


def _even_proj(x, g, w_all, bf_pad, width, heads, seq_len):
    rows, d = x.shape
    tm = ROW_TILE
    n_out = w_all.shape[1]
    channel_major = seq_len % tm == 0
    row_spec = lambda wd: pl.BlockSpec((tm, wd), lambda i: (i, 0))
    if channel_major:
        per_seq = seq_len // tm
        head_rows = pl.BlockSpec((None, width, tm), lambda i: (i // per_seq, 0, i % per_seq))
        kv_shape = jax.ShapeDtypeStruct((rows // seq_len, width, seq_len), F32)
    else:
        head_rows = pl.BlockSpec((tm * heads, FOX_HEAD_DIM), lambda i: (i, 0))
        kv_shape = jax.ShapeDtypeStruct((rows * heads, FOX_HEAD_DIM), F32)
    est = (2 * tm * d * 4 + w_all.size * 2 + 2 * tm * width * (3 * 2 + 4)
           + 2 * 2 * tm * heads * V7X_LANES * 4
           + 4 * tm * V7X_LANES * 4 + tm * n_out * 4 * 2 + tm * d * 2)
    return pl.pallas_call(
        functools.partial(_even_proj_body, width, heads, channel_major),
        grid=(rows // tm,),
        in_specs=[row_spec(d), _resident((1, d)), _resident(w_all.shape), _resident((1, V7X_LANES))],
        out_specs=[row_spec(width), row_spec(width), row_spec(width), head_rows, head_rows,
                   row_spec(width), pl.BlockSpec((heads, tm), lambda i: (0, i)), row_spec(V7X_LANES)],
        out_shape=[jax.ShapeDtypeStruct((rows, width), BF16),
                   jax.ShapeDtypeStruct((rows, width), BF16),
                   jax.ShapeDtypeStruct((rows, width), BF16),
                   kv_shape, kv_shape,
                   jax.ShapeDtypeStruct((rows, width), F32),
                   jax.ShapeDtypeStruct((heads, rows), F32),
                   jax.ShapeDtypeStruct((rows, V7X_LANES), F32)],
        compiler_params=pltpu.CompilerParams(
            dimension_semantics=("arbitrary",), vmem_limit_bytes=_vmem_limit(est)),
        name="even_projection",
    )(x, g.reshape(1, d), w_all, bf_pad)


def _attn_body(past, t_new, tq, heads, *refs):
    n_all = past + t_new
    n_pad = _round_up(n_all, V7X_LANES)
    if past:
        (q_ref, k_ref, v_ref, lf_ref, kp_ref, vp_ref, lfp_ref, u_ref, hist_ref, pw_ref, ps_ref,
         att_ref, pool_ref, ca, cb, c_hi, c_mid, c_lo, kcat, qcat, vcat, pa, pb) = refs
    else:
        (q_ref, k_ref, v_ref, lf_ref, u_ref, hist_ref, pw_ref, ps_ref,
         att_ref, pool_ref, ca, cb, c_hi, c_mid, c_lo, kcat, qcat, vcat, pa, pb) = refs
    pair = pl.program_id(1)

    @pl.when(pair == 0)
    def _():
        pad = 8
        ca[0:pad, :] = jnp.zeros((pad, V7X_LANES), F32)
        cb[0:pad, :] = jnp.zeros((pad, V7X_LANES), F32)
        if past:
            ca[pad:pad + past, :] = lfp_ref[...].T
        ca[pad + past:pad + n_all, :] = lf_ref[...]
        src, dst = ca, cb
        s = 1
        while s < n_all:
            if s < pad:
                dst[pad:pad + n_all, :] = src[pad:pad + n_all, :] + src[pad - s:pad + n_all - s, :]
            else:
                dst[pad:pad + s, :] = src[pad:pad + s, :]
                dst[pad + s:pad + n_all, :] = src[pad + s:pad + n_all, :] + src[pad:pad + n_all - s, :]
            src, dst = dst, src
            s *= 2
        c = src[pad:pad + n_all, :] * LOG2E
        hi = c.astype(BF16)
        r1 = c - hi.astype(F32)
        mid = r1.astype(BF16)
        c_hi[...] = hi
        c_mid[...] = mid
        c_lo[...] = (r1 - mid.astype(F32)).astype(BF16)
        vcat[0:n_all, V7X_LANES:2 * V7X_LANES] = jnp.ones((n_all, V7X_LANES), BF16)
        if n_pad > n_all:
            kcat[n_all:n_pad, :] = jnp.zeros((n_pad - n_all, V7X_LANES + AUG_LANES), BF16)
            vcat[n_all:n_pad, :] = jnp.zeros((n_pad - n_all, 2 * V7X_LANES), BF16)

    ln = lax.broadcasted_iota(jnp.int32, (1, AUG_LANES), 1)
    base = heads + 2 * GATE_COPIES * pair

    def pat(lanes, value=1.0):
        out = jnp.zeros((1, AUG_LANES), F32)
        for l in lanes:
            out = jnp.where(ln == base + l, value, out)
        return out.astype(BF16)

    k_lanes = lambda j: [hh * GATE_COPIES + N_SPLIT + j for hh in range(2)]
    k_pats = [pat(k_lanes(j), -1.0) for j in range(N_SPLIT)]
    k_ones = pat([hh * GATE_COPIES + j for hh in range(2) for j in range(N_SPLIT)])
    q_pats = [[pat([hh * GATE_COPIES + j]) for j in range(N_SPLIT)] for hh in range(2)]
    q_ones = [pat(range(hh * GATE_COPIES + N_SPLIT, (hh + 1) * GATE_COPIES)) for hh in range(2)]
    head_lanes = [jnp.where((ln >= hh * FOX_HEAD_DIM) & (ln < (hh + 1) * FOX_HEAD_DIM), 1.0, 0.0).astype(BF16)
                  for hh in range(2)]

    def stage_key_gates(lo, hi):
        kcat[lo:hi, V7X_LANES:V7X_LANES + AUG_LANES] = (
            c_hi[lo:hi, :] * k_pats[0] + c_mid[lo:hi, :] * k_pats[1] + c_lo[lo:hi, :] * k_pats[2] + k_ones)

    def stage_block(i):
        r0 = i * tq
        lo, hi = past + r0, past + r0 + tq
        stage_key_gates(lo, hi)
        kcat[lo:hi, 0:V7X_LANES] = k_ref[r0:r0 + tq, :]
        vcat[lo:hi, 0:V7X_LANES] = v_ref[r0:r0 + tq, :]
        qv = q_ref[r0:r0 + tq, :]
        for hh in range(2):
            aug_q = (c_hi[lo:hi, :] * q_pats[hh][0] + c_mid[lo:hi, :] * q_pats[hh][1]
                     + c_lo[lo:hi, :] * q_pats[hh][2] + q_ones[hh])
            qcat[i, hh * tq:(hh + 1) * tq, 0:V7X_LANES] = qv * head_lanes[hh]
            qcat[i, hh * tq:(hh + 1) * tq, V7X_LANES:V7X_LANES + AUG_LANES] = aug_q

    if past:
        kcat[0:past, 0:V7X_LANES] = kp_ref[...].T.astype(BF16)
        vcat[0:past, 0:V7X_LANES] = vp_ref[...].T.astype(BF16)
        stage_key_gates(0, past)

    def pool_group():
        lead = 2 * (POOL_HIST + 1)
        half = POOL_HIST + 1
        pa[0:half, :] = jnp.zeros((half, V7X_LANES), F32)
        pb[0:half, :] = jnp.zeros((half, V7X_LANES), F32)
        pa[half:lead, :] = hist_ref[...]
        pa[lead:lead + t_new, :] = u_ref[...]
        src, dst = pa, pb
        win = None
        shift = 1
        for gi in range(len(POOL_WINDOWS)):
            dst[half:lead + t_new, :] = src[half:lead + t_new, :] + src[half - shift:lead + t_new - shift, :]
            cur = dst[lead:lead + t_new, :]
            win = cur if win is None else jnp.where(pair >= gi, cur, win)
            src, dst = dst, src
            shift *= 2
        width = jnp.left_shift(2, pair)
        posn = lax.broadcasted_iota(jnp.int32, (t_new, V7X_LANES), 0) + past
        cnt = jnp.minimum(width, posn + 1).astype(F32)
        return (win / cnt - u_ref[...]).astype(BF16)

    def pool_mix(dlt):
        y = jnp.dot(dlt, pw_ref[...], preferred_element_type=F32)
        pool_ref[...] = (y * ps_ref[...]).astype(BF16)

    n_blocks = t_new // tq
    nt = (((1,), (1,)), ((), ()))
    lane_o = lax.broadcasted_iota(jnp.int32, (tq, V7X_LANES), 1)

    def scores(i):
        r0 = i * tq
        kend = _round_up(past + r0 + tq, V7X_LANES)
        d0 = (past + r0) // V7X_LANES * V7X_LANES
        qs = qcat[i]
        bounds = [(c0, min(c0 + ATT_KV_CHUNK, kend)) for c0 in range(0, kend, ATT_KV_CHUNK)]
        s_list = []
        mx = None
        for c0, c1 in bounds:
            s = lax.dot_general(qs, kcat[c0:c1, :], nt, preferred_element_type=F32)
            if c1 > d0:
                qpos = jnp.bitwise_and(lax.broadcasted_iota(jnp.int32, (2 * tq, c1 - c0), 0), tq - 1) + (past + r0)
                kpos = lax.broadcasted_iota(jnp.int32, (2 * tq, c1 - c0), 1) + c0
                s = jnp.where(kpos <= qpos, s, -jnp.inf)
            s_list.append(s)
            fold = s[:, 0:V7X_LANES]
            for l0 in range(V7X_LANES, c1 - c0, V7X_LANES):
                fold = jnp.maximum(fold, s[:, l0:l0 + V7X_LANES])
            mx = fold if mx is None else jnp.maximum(mx, fold)
        return s_list, bounds, jnp.max(mx, axis=1, keepdims=True)

    def finish(i, s_list, bounds, m):
        o = None
        for s, (c0, c1) in zip(s_list, bounds):
            pv = jnp.dot(jnp.exp2(s - m).astype(BF16), vcat[c0:c1, :], preferred_element_type=F32)
            o = pv if o is None else o + pv
        o = o[:, 0:V7X_LANES] / o[:, V7X_LANES:2 * V7X_LANES]
        att_ref[i * tq:(i + 1) * tq, :] = jnp.where(lane_o < FOX_HEAD_DIM, o[0:tq], o[tq:2 * tq]).astype(BF16)

    def staged_scores(i):
        stage_block(i)
        return scores(i)

    pending = [staged_scores(i) for i in range(min(ATT_LOOKAHEAD, n_blocks))]
    for i in range(n_blocks):
        if i + ATT_LOOKAHEAD < n_blocks:
            pending.append(staged_scores(i + ATT_LOOKAHEAD))
        finish(i, *pending.pop(0))
    pool_mix(pool_group())


def _attention_pool(heads, q, k, v, logf_wide, u, hist16, pool_w, pool_scale, past_kvf=None):
    b, t_new, width = q.shape
    n_pairs = width // V7X_LANES
    assert heads + heads * GATE_COPIES <= AUG_LANES
    past = 0 if past_kvf is None else past_kvf[0].shape[2]
    n_all = past + t_new
    n_pad = _round_up(n_all, V7X_LANES)
    tq = min(ATT_Q_BLOCK, t_new)
    assert tq & (tq - 1) == 0 and t_new % tq == 0 and heads % 2 == 0

    slab = lambda rows: pl.BlockSpec((None, rows, V7X_LANES), lambda bi, pi: (bi, 0, pi))
    gates = lambda rows: pl.BlockSpec((None, rows, V7X_LANES), lambda bi, pi: (bi, 0, 0))
    in_specs = [slab(t_new), slab(t_new), slab(t_new), gates(t_new)]
    args = [q, k, v, logf_wide]
    if past:
        channel_major = pl.BlockSpec((None, V7X_LANES, past), lambda bi, pi: (bi, pi, 0))
        in_specs += [channel_major, channel_major,
                     pl.BlockSpec((None, V7X_LANES, past), lambda bi, pi: (bi, 0, 0))]
        args += list(past_kvf)
    in_specs += [slab(t_new), slab(POOL_HIST + 1),
                 pl.BlockSpec((None, V7X_LANES, V7X_LANES), lambda bi, pi: (pi, 0, 0)),
                 pl.BlockSpec((None, 1, V7X_LANES), lambda bi, pi: (pi, 0, 0))]
    args += [u, hist16, pool_w, pool_scale]

    lane_pad_f32 = V7X_LANES * 4
    scratch = [
        pltpu.VMEM((8 + n_all, V7X_LANES), F32), pltpu.VMEM((8 + n_all, V7X_LANES), F32),
        pltpu.VMEM((n_all, AUG_LANES), BF16), pltpu.VMEM((n_all, AUG_LANES), BF16),
        pltpu.VMEM((n_all, AUG_LANES), BF16),
        pltpu.VMEM((n_pad, V7X_LANES + AUG_LANES), BF16),
        pltpu.VMEM((t_new // tq, 2 * tq, V7X_LANES + AUG_LANES), BF16),
        pltpu.VMEM((n_pad, 2 * V7X_LANES), BF16),
        pltpu.VMEM((2 * (POOL_HIST + 1) + t_new, V7X_LANES), F32),
        pltpu.VMEM((2 * (POOL_HIST + 1) + t_new, V7X_LANES), F32),
    ]
    est = (2 * t_new * V7X_LANES * (2 + 2 + 2 + 4 + 2 + 2) + 2 * t_new * lane_pad_f32
           + 2 * past * (2 * V7X_LANES * 4 + lane_pad_f32)
           + 3 * (8 + n_all) * lane_pad_f32 + n_all * lane_pad_f32
           + 2 * n_pad * (V7X_LANES + AUG_LANES) * 2 + 2 * t_new * (V7X_LANES + AUG_LANES) * 2
           + 2 * (32 + t_new) * lane_pad_f32
           + 4 * 2 * tq * n_pad * 4 + 10 * n_all * lane_pad_f32)
    return pl.pallas_call(
        functools.partial(_attn_body, past, t_new, tq, heads),
        grid=(b, n_pairs),
        in_specs=in_specs,
        out_specs=[slab(t_new), slab(t_new)],
        out_shape=[jax.ShapeDtypeStruct((b, t_new, width), BF16),
                   jax.ShapeDtypeStruct((b, t_new, width), BF16)],
        scratch_shapes=scratch,
        compiler_params=pltpu.CompilerParams(
            dimension_semantics=("arbitrary", "arbitrary"), vmem_limit_bytes=_vmem_limit(est)),
        name="fox_attention_pool",
    )(*args)


def _sgu_body(groups, want_zv, x_ref, g_ref, w_ref, ng_ref, ws_ref, bias_ref, *rest):
    if want_zv:
        o_ref, zv_ref, zvb_scr = rest
    else:
        o_ref, zvb_scr = rest
        zv_ref = None
    width = ng_ref.shape[1]
    gdim = width // groups
    starts = range(0, x_ref.shape[0], SUB_TILE)
    zs = [jnp.dot(_rms(x_ref[r0:r0 + SUB_TILE, :], g_ref[...]).astype(BF16), w_ref[...],
                  preferred_element_type=F32) for r0 in starts]
    for r0, z in zip(starts, zs):
        rows = slice(r0, r0 + SUB_TILE)
        half_z = 0.5 * z
        z = half_z + half_z * jnp.tanh(z * (GELU_A + (GELU_A * GELU_B) * (z * z)))
        zv = _rms(z[:, width:], ng_ref[...])
        if want_zv:
            zv_ref[rows, :] = zv
        zvb_scr[rows, :] = zv.astype(BF16)
        for ch in range(SUB_TILE // SGU_CHUNK):
            rs = slice(ch * SGU_CHUNK, (ch + 1) * SGU_CHUNK)
            out_rs = slice(r0 + ch * SGU_CHUNK, r0 + (ch + 1) * SGU_CHUNK)
            for gi in range(groups):
                cs = slice(gi * gdim, (gi + 1) * gdim)
                mix = jnp.dot(ws_ref[gi], zvb_scr[out_rs, cs], preferred_element_type=F32) + bias_ref[:, cs]
                o_ref[out_rs, cs] = (z[rs, cs] * mix).astype(BF16)


def _sgu(x, g, w_in, norm_g, ws_mat, bias_full, want_zv):
    rows, d = x.shape
    width = norm_g.shape[0]
    groups = ws_mat.shape[0]
    tm = ROW_TILE
    row_spec = lambda wd: pl.BlockSpec((tm, wd), lambda i: (i, 0))
    out_specs = [row_spec(width)]
    out_shape = [jax.ShapeDtypeStruct((rows, width), BF16)]
    if want_zv:
        out_specs.append(row_spec(width))
        out_shape.append(jax.ShapeDtypeStruct((rows, width), F32))
    est = (2 * tm * d * 4 + w_in.size * 2 + ws_mat.size * 2 + bias_full.size * 4
           + 2 * tm * width * 2 + (2 * tm * width * 4 if want_zv else 0)
           + tm * width * 2 + 3 * tm * 2 * width * 4)
    res = pl.pallas_call(
        functools.partial(_sgu_body, groups, want_zv),
        grid=(rows // tm,),
        in_specs=[row_spec(d), _resident((1, d)), _resident(w_in.shape), _resident((1, width)),
                  _resident(ws_mat.shape), _resident(bias_full.shape)],
        out_specs=out_specs,
        out_shape=out_shape,
        scratch_shapes=[pltpu.VMEM((tm, width), BF16)],
        compiler_params=pltpu.CompilerParams(
            dimension_semantics=("arbitrary",), vmem_limit_bytes=_vmem_limit(est)),
        name="spatial_gating",
    )(x, g.reshape(1, d), w_in, norm_g.reshape(1, width), ws_mat, bias_full)
    return res if want_zv else (res[0], None)


def _sgu_spatial(w_s, b_s, seq_len):
    groups = w_s.shape[0]
    span = min(seq_len, SGU_CHUNK)
    idx = jnp.arange(span) // SGU_STREAM_CHUNK
    w = jnp.where((idx[None, :] <= idx[:, None])[None], w_s[:, :span, :span], 0.0)
    reps = SGU_CHUNK // span
    if reps > 1:
        w = jnp.einsum('ab,gts->gatbs', jnp.eye(reps, dtype=w.dtype), w).reshape(groups, SGU_CHUNK, SGU_CHUNK)
    bias = jnp.tile(b_s[:, :span], (1, reps))
    return w.astype(BF16), bias


def _per_head(kv, batch, seq_len, heads):
    if kv.ndim == 3:
        return jnp.transpose(kv.reshape(batch, heads, FOX_HEAD_DIM, seq_len), (0, 3, 1, 2))
    return kv.reshape(batch, seq_len, heads, FOX_HEAD_DIM)


def kernel(x_prompt, x_sample, cache_k, cache_v, cache_logf, state_pool, norm_g, ffn_w_in, ffn_w_down,
           even_w_in, even_b_f, pool_w, pool_scale, even_w_out, sgu_w_in, sgu_norm_g, sgu_w_s, sgu_b_s,
           sgu_w_out, final_g):
    b, s, d = x_prompt.shape
    bs, t, _ = x_sample.shape
    depth = norm_g.shape[0]
    heads = even_b_f.shape[1]
    width = heads * FOX_HEAD_DIM
    pool_width = pool_scale.shape[1]
    n_groups = pool_w.shape[1]
    past = cache_k.shape[2]
    sgu_width = sgu_norm_g.shape[1]
    sgu_groups = sgu_w_s.shape[1]
    sgu_gdim = sgu_width // sgu_groups

    xp = x_prompt.reshape(b * s, d)
    xs = x_sample.reshape(bs * t, d)
    kp_l, vp_l, fp_l, up_l = [], [], [], []
    ks_l, vs_l, fs_l, us_l, zs_l = [], [], [], [], []

    w_in_all = ffn_w_in.astype(BF16)
    w_down_all = ffn_w_down.astype(BF16)
    for l in range(depth):
        last = final_g if l == depth - 1 else None
        ffn_b = functools.partial(_ffn, g=norm_g[l, 2], w_in_all=w_in_all, w_down_all=w_down_all,
                                  which=(l, 1), final_g=last)
        xp = _ffn(xp, [], norm_g[l, 0], w_in_all, w_down_all, (l, 0))
        xs = _ffn(xs, [], norm_g[l, 0], w_in_all, w_down_all, (l, 0))
        if l % 2 == 0:
            e = l // 2
            w_e = even_w_in[e]
            f_cols = _replicate_gates(w_e[:, 3 * width:3 * width + heads], heads)
            w_all = jnp.concatenate([w_e[:, :3 * width], w_e[:, 3 * width + heads:], f_cols], axis=1).astype(BF16)
            bf_pad = _replicate_gates(even_b_f[e], heads).reshape(1, V7X_LANES)
            pw = pool_w[e].astype(BF16)
            psc = pool_scale[e].reshape(n_groups, 1, pool_width // n_groups)
            w_out_att = even_w_out[e, :width].astype(BF16)
            w_out_pool = even_w_out[e, width:].astype(BF16)

            q, kb, vb, k, v, u, lf, lfw = _even_proj(xp, norm_g[l, 1], w_all, bf_pad, width, heads, s)
            hist = jnp.zeros((b, POOL_HIST + 1, pool_width), F32)
            att, pool = _attention_pool(heads, q.reshape(b, s, width), kb.reshape(b, s, width),
                                        vb.reshape(b, s, width), lfw.reshape(b, s, V7X_LANES),
                                        u.reshape(b, s, pool_width), hist, pw, psc)
            xp = ffn_b(xp, [(att.reshape(b * s, width), w_out_att), (pool.reshape(b * s, pool_width), w_out_pool)])
            kp_l.append(_per_head(k, b, s, heads))
            vp_l.append(_per_head(v, b, s, heads))
            fp_l.append(jnp.transpose(lf.reshape(heads, b, s), (1, 2, 0)))
            up_l.append(u.reshape(b, s, pool_width)[:, s - POOL_HIST:])

            q, kb, vb, k, v, u, lf, lfw = _even_proj(xs, norm_g[l, 1], w_all, bf_pad, width, heads, t)
            hist = jnp.pad(state_pool[e], ((0, 0), (1, 0), (0, 0)))
            channel_major = lambda c: jnp.transpose(c, (0, 2, 3, 1)).reshape(bs, width, past)
            past_kvf = (channel_major(cache_k[e]), channel_major(cache_v[e]),
                        _replicate_gates(jnp.transpose(cache_logf[e], (0, 2, 1)), heads, axis=1))
            att, pool = _attention_pool(heads, q.reshape(bs, t, width), kb.reshape(bs, t, width),
                                        vb.reshape(bs, t, width), lfw.reshape(bs, t, V7X_LANES),
                                        u.reshape(bs, t, pool_width), hist, pw, psc, past_kvf)
            xs = ffn_b(xs, [(att.reshape(bs * t, width), w_out_att), (pool.reshape(bs * t, pool_width), w_out_pool)])
            ks_l.append(_per_head(k, bs, t, heads))
            vs_l.append(_per_head(v, bs, t, heads))
            fs_l.append(jnp.transpose(lf.reshape(heads, bs, t), (1, 2, 0)))
            u_ext = jnp.concatenate([state_pool[e], u.reshape(bs, t, pool_width)], axis=1)
            us_l.append(u_ext[:, -POOL_HIST:])
        else:
            o = l // 2
            w_in = sgu_w_in[o].astype(BF16)
            w_out = sgu_w_out[o].astype(BF16)
            ws_p, bias_p = _sgu_spatial(sgu_w_s[o], sgu_b_s[o], s)
            ws_s, bias_s = _sgu_spatial(sgu_w_s[o], sgu_b_s[o], t)
            expand = lambda bias: jnp.repeat(bias.T, sgu_gdim, axis=1)
            gated, _ = _sgu(xp, norm_g[l, 1], w_in, sgu_norm_g[o], ws_p, expand(bias_p), False)
            xp = ffn_b(xp, [(gated, w_out)])
            gated, zv = _sgu(xs, norm_g[l, 1], w_in, sgu_norm_g[o], ws_s, expand(bias_s), True)
            xs = ffn_b(xs, [(gated, w_out)])
            zs_l.append(zv.reshape(bs, t, sgu_width))

    return (xp.reshape(b, s, d), xs.reshape(bs, t, d),
            jnp.stack(kp_l), jnp.stack(vp_l), jnp.stack(fp_l), jnp.stack(up_l),
            jnp.stack(ks_l), jnp.stack(vs_l), jnp.stack(fs_l), jnp.stack(us_l), jnp.stack(zs_l))
```

```python
import functools
import math

import jax
import jax.numpy as jnp
from jax import lax
from jax.experimental import pallas as pl
from jax.experimental.pallas import tpu as pltpu

F32 = jnp.float32
BF16 = jnp.bfloat16

EPS = 1e-6
FOX_HEAD_DIM = 64
POOL_WINDOWS = (2, 4, 8, 16)
POOL_HIST = max(POOL_WINDOWS) - 1
SGU_STREAM_CHUNK = 64
SGU_CHUNK = 128

V7X_LANES = 128
V7X_MXU_DIM = 256
V7X_VMEM_BYTES = 64 * 1024 * 1024
VMEM_COMPILER_RESERVE = 8 * 1024 * 1024
MIN_VMEM_LIMIT = 16 * 1024 * 1024
V7X_SUBLANES = 8

ROW_TILE = 1024
SUB_TILE = 512
FF_CHUNK = V7X_MXU_DIM
ATT_Q_BLOCK = 256
ATT_KV_CHUNK = V7X_MXU_DIM
ATT_LOOKAHEAD = 1
AUG_LANES = V7X_LANES
N_SPLIT = 3
GATE_COPIES = 2 * N_SPLIT
LOG2E = math.log2(math.e)
GELU_A = math.sqrt(2.0 / math.pi)
GELU_B = 0.044715


def _vmem_limit(est_bytes):
    return int(min(max(est_bytes, MIN_VMEM_LIMIT), V7X_VMEM_BYTES - VMEM_COMPILER_RESERVE))


def _resident(shape):
    nd = len(shape)
    return pl.BlockSpec(shape, lambda *_: (0,) * nd, pipeline_mode=pl.Buffered(1))


def _rms(x, g):
    ms = jnp.mean(x * x, axis=-1, keepdims=True)
    return x * lax.rsqrt(ms + EPS) * g


def _round_up(n, m):
    return (n + m - 1) // m * m


def _ffn_body(n_mix, n_chunks, has_final, *refs):
    x_ref = refs[0]
    a_refs = refs[1:1 + n_mix]
    w_refs = refs[1 + n_mix:1 + 2 * n_mix]
    g_ref, win_ref, wdown_ref = refs[1 + 2 * n_mix:4 + 2 * n_mix]
    pos = 4 + 2 * n_mix
    fg_ref = refs[pos] if has_final else None
    pos += int(has_final)
    o_ref, h_scr, act_scr = refs[pos:pos + 3]

    d_ff = n_chunks * FF_CHUNK
    for r0 in range(0, x_ref.shape[0], SUB_TILE):
        rows = slice(r0, r0 + SUB_TILE)
        x = x_ref[rows, :]
        for a_ref, w_ref in zip(a_refs, w_refs):
            x = x + jnp.dot(a_ref[rows, :], w_ref[...], preferred_element_type=F32)
        o_ref[rows, :] = x
        h_scr[rows, :] = _rms(x, g_ref[...]).astype(BF16)
        for j in range(n_chunks):
            cols = slice(j * FF_CHUNK, (j + 1) * FF_CHUNK)
            up_cols = slice(d_ff + j * FF_CHUNK, d_ff + (j + 1) * FF_CHUNK)
            gate = jnp.dot(h_scr[rows, :], win_ref[:, cols], preferred_element_type=F32)
            up = jnp.dot(h_scr[rows, :], win_ref[:, up_cols], preferred_element_type=F32)
            act = gate * (1.0 / (1.0 + jnp.exp(-gate))) * up
            act_scr[rows, cols] = act.astype(BF16)
        y = jnp.dot(act_scr[rows, :], wdown_ref[...], preferred_element_type=F32)
        out = o_ref[rows, :] + 0.5 * y
        if has_final:
            out = _rms(out, fg_ref[...])
        o_ref[rows, :] = out


def _ffn(x, mixes, g, w_in_all, w_down_all, which, final_g=None):
    rows, d = x.shape
    d_ff = w_down_all.shape[2]
    assert w_in_all.shape[2:] == (d, 2 * d_ff) and d_ff % FF_CHUNK == 0
    n_chunks = d_ff // FF_CHUNK
    picked = lambda shape: pl.BlockSpec((None, None) + shape, lambda i: which + (0, 0),
                                        pipeline_mode=pl.Buffered(1))
    two_chunk = 2 * FF_CHUNK
    tm = ROW_TILE
    n_mix = len(mixes)
    has_final = final_g is not None

    row_spec = lambda width: pl.BlockSpec((tm, width), lambda i: (i, 0))
    in_specs = [row_spec(d)]
    in_specs += [row_spec(a.shape[1]) for a, _ in mixes]
    in_specs += [_resident(w.shape) for _, w in mixes]
    in_specs += [_resident((1, d)), picked((d, 2 * d_ff)), picked((d_ff, d))]
    args = [x] + [a for a, _ in mixes] + [w for _, w in mixes] + [g.reshape(1, d), w_in_all, w_down_all]
    if has_final:
        in_specs.append(_resident((1, d)))
        args.append(final_g.reshape(1, d))

    est = (4 * tm * d * 4
           + sum(2 * tm * a.shape[1] * 2 + w.size * 2 for a, w in mixes)
           + 3 * d * d_ff * 2
           + tm * d * 2 + tm * d_ff * 2
           + SUB_TILE * two_chunk * 4 * 2 + SUB_TILE * d * 4 * 2)
    return pl.pallas_call(
        functools.partial(_ffn_body, n_mix, n_chunks, has_final),
        grid=(rows // tm,),
        in_specs=in_specs,
        out_specs=row_spec(d),
        out_shape=jax.ShapeDtypeStruct((rows, d), F32),
        scratch_shapes=[pltpu.VMEM((tm, d), BF16), pltpu.VMEM((tm, d_ff), BF16)],
        compiler_params=pltpu.CompilerParams(
            dimension_semantics=("arbitrary",), vmem_limit_bytes=_vmem_limit(est)),
        name="ffn_half_step",
    )(*args)


def _even_proj_body(width, heads, channel_major, x_ref, g_ref, w_ref, bf_ref,
                    q_ref, kb_ref, vb_ref, k4_ref, v4_ref, u_ref, lf_ref, lfw_ref):
    starts = range(0, x_ref.shape[0], SUB_TILE)
    zs = [jnp.dot(_rms(x_ref[r0:r0 + SUB_TILE, :], g_ref[...]).astype(BF16), w_ref[...],
                  preferred_element_type=F32) for r0 in starts]
    for r0, z in zip(starts, zs):
        rows = slice(r0, r0 + SUB_TILE)
        q_ref[rows, :] = (z[:, :width] * (FOX_HEAD_DIM ** -0.5 * LOG2E)).astype(BF16)
        kb_ref[rows, :] = z[:, width:2 * width].astype(BF16)
        vb_ref[rows, :] = z[:, 2 * width:3 * width].astype(BF16)
        if channel_major:
            k4_ref[:, rows] = z[:, width:2 * width].T
            v4_ref[:, rows] = z[:, 2 * width:3 * width].T
        else:
            for hd in range(heads):
                head_rows = pl.ds(r0 * heads + hd, SUB_TILE, stride=heads)
                k4_ref[head_rows, :] = z[:, width + hd * FOX_HEAD_DIM:width + (hd + 1) * FOX_HEAD_DIM]
                v4_ref[head_rows, :] = z[:, 2 * width + hd * FOX_HEAD_DIM:2 * width + (hd + 1) * FOX_HEAD_DIM]
        u_ref[rows, :] = z[:, 3 * width:4 * width]
        f = z[:, 4 * width:4 * width + V7X_LANES] + bf_ref[...]
        logf = jnp.minimum(f, 0.0) - jnp.log(1.0 + jnp.exp(-jnp.abs(f)))
        lf_ref[:, rows] = logf.T[:heads, :]
        lfw_ref[rows, :] = logf


def _replicate_gates(cols, heads, axis=-1):
    axis = axis % cols.ndim
    src = jnp.array([h for h in range(heads) for _ in range(GATE_COPIES)])
    pad_shape = cols.shape[:axis] + (V7X_LANES - heads * (1 + GATE_COPIES),) + cols.shape[axis + 1:]
    return jnp.concatenate([cols, jnp.take(cols, src, axis=axis), jnp.zeros(pad_shape, cols.dtype)], axis=axis)


def _even_proj(x, g, w_all, bf_pad, width, heads, seq_len):
    rows, d = x.shape
    tm = ROW_TILE
    n_out = w_all.shape[1]
    channel_major = seq_len % tm == 0
    row_spec = lambda wd: pl.BlockSpec((tm, wd), lambda i: (i, 0))
    if channel_major:
        per_seq = seq_len // tm
        head_rows = pl.BlockSpec((None, width, tm), lambda i: (i // per_seq, 0, i % per_seq))
        kv_shape = jax.ShapeDtypeStruct((rows // seq_len, width, seq_len), F32)
    else:
        head_rows = pl.BlockSpec((tm * heads, FOX_HEAD_DIM), lambda i: (i, 0))
        kv_shape = jax.ShapeDtypeStruct((rows * heads, FOX_HEAD_DIM), F32)
    est = (2 * tm * d * 4 + w_all.size * 2 + 2 * tm * width * (3 * 2 + 4)
           + 2 * 2 * tm * heads * V7X_LANES * 4
           + 4 * tm * V7X_LANES * 4 + tm * n_out * 4 * 2 + tm * d * 2)
    return pl.pallas_call(
        functools.partial(_even_proj_body, width, heads, channel_major),
        grid=(rows // tm,),
        in_specs=[row_spec(d), _resident((1, d)), _resident(w_all.shape), _resident((1, V7X_LANES))],
        out_specs=[row_spec(width), row_spec(width), row_spec(width), head_rows, head_rows,
                   row_spec(width), pl.BlockSpec((heads, tm), lambda i: (0, i)), row_spec(V7X_LANES)],
        out_shape=[jax.ShapeDtypeStruct((rows, width), BF16),
                   jax.ShapeDtypeStruct((rows, width), BF16),
                   jax.ShapeDtypeStruct((rows, width), BF16),
                   kv_shape, kv_shape,
                   jax.ShapeDtypeStruct((rows, width), F32),
                   jax.ShapeDtypeStruct((heads, rows), F32),
                   jax.ShapeDtypeStruct((rows, V7X_LANES), F32)],
        compiler_params=pltpu.CompilerParams(
            dimension_semantics=("arbitrary",), vmem_limit_bytes=_vmem_limit(est)),
        name="even_projection",
    )(x, g.reshape(1, d), w_all, bf_pad)


def _attn_body(past, t_new, tq, heads, *refs):
    n_all = past + t_new
    n_pad = _round_up(n_all, V7X_LANES)
    if past:
        (q_ref, k_ref, v_ref, lf_ref, kp_ref, vp_ref, lfp_ref, u_ref, hist_ref, pw_ref, ps_ref,
         att_ref, pool_ref, ca, cb, c_hi, c_mid, c_lo, kcat, qcat, vcat, pa, pb) = refs
    else:
        (q_ref, k_ref, v_ref, lf_ref, u_ref, hist_ref, pw_ref, ps_ref,
         att_ref, pool_ref, ca, cb, c_hi, c_mid, c_lo, kcat, qcat, vcat, pa, pb) = refs
    pair = pl.program_id(1)

    @pl.when(pair == 0)
    def _():
        pad = V7X_SUBLANES
        ca[0:pad, :] = jnp.zeros((pad, V7X_LANES), F32)
        cb[0:pad, :] = jnp.zeros((pad, V7X_LANES), F32)
        if past:
            ca[pad:pad + past, :] = lfp_ref[...].T
        ca[pad + past:pad + n_all, :] = lf_ref[...]
        src, dst = ca, cb
        s = 1
        while s < n_all:
            if s < pad:
                dst[pad:pad + n_all, :] = src[pad:pad + n_all, :] + src[pad - s:pad + n_all - s, :]
            else:
                dst[pad:pad + s, :] = src[pad:pad + s, :]
                dst[pad + s:pad + n_all, :] = src[pad + s:pad + n_all, :] + src[pad:pad + n_all - s, :]
            src, dst = dst, src
            s *= 2
        c = src[pad:pad + n_all, :] * LOG2E
        hi = c.astype(BF16)
        r1 = c - hi.astype(F32)
        mid = r1.astype(BF16)
        c_hi[...] = hi
        c_mid[...] = mid
        c_lo[...] = (r1 - mid.astype(F32)).astype(BF16)
        vcat[0:n_all, V7X_LANES:2 * V7X_LANES] = jnp.ones((n_all, V7X_LANES), BF16)
        if n_pad > n_all:
            kcat[n_all:n_pad, :] = jnp.zeros((n_pad - n_all, V7X_LANES + AUG_LANES), BF16)
            vcat[n_all:n_pad, :] = jnp.zeros((n_pad - n_all, 2 * V7X_LANES), BF16)

    ln = lax.broadcasted_iota(jnp.int32, (1, AUG_LANES), 1)
    base = heads + 2 * GATE_COPIES * pair

    def pat(lanes, value=1.0):
        out = jnp.zeros((1, AUG_LANES), F32)
        for l in lanes:
            out = jnp.where(ln == base + l, value, out)
        return out.astype(BF16)

    k_lanes = lambda j: [hh * GATE_COPIES + N_SPLIT + j for hh in range(2)]
    k_pats = [pat(k_lanes(j), -1.0) for j in range(N_SPLIT)]
    k_ones = pat([hh * GATE_COPIES + j for hh in range(2) for j in range(N_SPLIT)])
    q_pats = [[pat([hh * GATE_COPIES + j]) for j in range(N_SPLIT)] for hh in range(2)]
    q_ones = [pat(range(hh * GATE_COPIES + N_SPLIT, (hh + 1) * GATE_COPIES)) for hh in range(2)]
    head_lanes = [jnp.where((ln >= hh * FOX_HEAD_DIM) & (ln < (hh + 1) * FOX_HEAD_DIM), 1.0, 0.0).astype(BF16)
                  for hh in range(2)]

    def stage_key_gates(lo, hi):
        kcat[lo:hi, V7X_LANES:V7X_LANES + AUG_LANES] = (
            c_hi[lo:hi, :] * k_pats[0] + c_mid[lo:hi, :] * k_pats[1] + c_lo[lo:hi, :] * k_pats[2] + k_ones)

    def stage_block(i):
        r0 = i * tq
        lo, hi = past + r0, past + r0 + tq
        stage_key_gates(lo, hi)
        kcat[lo:hi, 0:V7X_LANES] = k_ref[r0:r0 + tq, :]
        vcat[lo:hi, 0:V7X_LANES] = v_ref[r0:r0 + tq, :]
        qv = q_ref[r0:r0 + tq, :]
        for hh in range(2):
            aug_q = (c_hi[lo:hi, :] * q_pats[hh][0] + c_mid[lo:hi, :] * q_pats[hh][1]
                     + c_lo[lo:hi, :] * q_pats[hh][2] + q_ones[hh])
            qcat[i, hh * tq:(hh + 1) * tq, 0:V7X_LANES] = qv * head_lanes[hh]
            qcat[i, hh * tq:(hh + 1) * tq, V7X_LANES:V7X_LANES + AUG_LANES] = aug_q

    if past:
        kcat[0:past, 0:V7X_LANES] = kp_ref[...].T.astype(BF16)
        vcat[0:past, 0:V7X_LANES] = vp_ref[...].T.astype(BF16)
        stage_key_gates(0, past)

    def pool_group():
        lead = 2 * (POOL_HIST + 1)
        half = POOL_HIST + 1
        pa[0:half, :] = jnp.zeros((half, V7X_LANES), F32)
        pb[0:half, :] = jnp.zeros((half, V7X_LANES), F32)
        pa[half:lead, :] = hist_ref[...]
        pa[lead:lead + t_new, :] = u_ref[...]
        src, dst = pa, pb
        win = None
        shift = 1
        for gi in range(len(POOL_WINDOWS)):
            dst[half:lead + t_new, :] = src[half:lead + t_new, :] + src[half - shift:lead + t_new - shift, :]
            cur = dst[lead:lead + t_new, :]
            win = cur if win is None else jnp.where(pair >= gi, cur, win)
            src, dst = dst, src
            shift *= 2
        width = jnp.left_shift(2, pair)
        posn = lax.broadcasted_iota(jnp.int32, (t_new, V7X_LANES), 0) + past
        cnt = jnp.minimum(width, posn + 1).astype(F32)
        return (win / cnt - u_ref[...]).astype(BF16)

    def pool_mix(dlt):
        y = jnp.dot(dlt, pw_ref[...], preferred_element_type=F32)
        pool_ref[...] = (y * ps_ref[...]).astype(BF16)

    n_blocks = t_new // tq
    nt = (((1,), (1,)), ((), ()))
    lane_o = lax.broadcasted_iota(jnp.int32, (tq, V7X_LANES), 1)

    def scores(i):
        r0 = i * tq
        kend = _round_up(past + r0 + tq, V7X_LANES)
        d0 = (past + r0) // V7X_LANES * V7X_LANES
        qs = qcat[i]
        bounds = [(c0, min(c0 + ATT_KV_CHUNK, kend)) for c0 in range(0, kend, ATT_KV_CHUNK)]
        s_list = []
        mx = None
        for c0, c1 in bounds:
            s = lax.dot_general(qs, kcat[c0:c1, :], nt, preferred_element_type=F32)
            if c1 > d0:
                qpos = jnp.bitwise_and(lax.broadcasted_iota(jnp.int32, (2 * tq, c1 - c0), 0), tq - 1) + (past + r0)
                kpos = lax.broadcasted_iota(jnp.int32, (2 * tq, c1 - c0), 1) + c0
                s = jnp.where(kpos <= qpos, s, -jnp.inf)
            s_list.append(s)
            fold = s[:, 0:V7X_LANES]
            for l0 in range(V7X_LANES, c1 - c0, V7X_LANES):
                fold = jnp.maximum(fold, s[:, l0:l0 + V7X_LANES])
            mx = fold if mx is None else jnp.maximum(mx, fold)
        return s_list, bounds, jnp.max(mx, axis=1, keepdims=True)

    def finish(i, s_list, bounds, m):
        o = None
        for s, (c0, c1) in zip(s_list, bounds):
            pv = jnp.dot(jnp.exp2(s - m).astype(BF16), vcat[c0:c1, :], preferred_element_type=F32)
            o = pv if o is None else o + pv
        o = o[:, 0:V7X_LANES] / o[:, V7X_LANES:2 * V7X_LANES]
        att_ref[i * tq:(i + 1) * tq, :] = jnp.where(lane_o < FOX_HEAD_DIM, o[0:tq], o[tq:2 * tq]).astype(BF16)

    def staged_scores(i):
        stage_block(i)
        return scores(i)

    pending = [staged_scores(i) for i in range(min(ATT_LOOKAHEAD, n_blocks))]
    for i in range(n_blocks):
        if i + ATT_LOOKAHEAD < n_blocks:
            pending.append(staged_scores(i + ATT_LOOKAHEAD))
        finish(i, *pending.pop(0))
    pool_mix(pool_group())


def _attention_pool(heads, q, k, v, logf_wide, u, hist16, pool_w, pool_scale, past_kvf=None):
    b, t_new, width = q.shape
    n_pairs = width // V7X_LANES
    assert heads + heads * GATE_COPIES <= AUG_LANES
    past = 0 if past_kvf is None else past_kvf[0].shape[2]
    n_all = past + t_new
    n_pad = _round_up(n_all, V7X_LANES)
    tq = min(ATT_Q_BLOCK, t_new)
    assert tq & (tq - 1) == 0 and t_new % tq == 0 and heads % 2 == 0

    slab = lambda rows: pl.BlockSpec((None, rows, V7X_LANES), lambda bi, pi: (bi, 0, pi))
    gates = lambda rows: pl.BlockSpec((None, rows, V7X_LANES), lambda bi, pi: (bi, 0, 0))
    in_specs = [slab(t_new), slab(t_new), slab(t_new), gates(t_new)]
    args = [q, k, v, logf_wide]
    if past:
        channel_major = pl.BlockSpec((None, V7X_LANES, past), lambda bi, pi: (bi, pi, 0))
        in_specs += [channel_major, channel_major,
                     pl.BlockSpec((None, V7X_LANES, past), lambda bi, pi: (bi, 0, 0))]
        args += list(past_kvf)
    in_specs += [slab(t_new), slab(POOL_HIST + 1),
                 pl.BlockSpec((None, V7X_LANES, V7X_LANES), lambda bi, pi: (pi, 0, 0)),
                 pl.BlockSpec((None, 1, V7X_LANES), lambda bi, pi: (pi, 0, 0))]
    args += [u, hist16, pool_w, pool_scale]

    lane_pad_f32 = V7X_LANES * 4
    scratch = [
        pltpu.VMEM((V7X_SUBLANES + n_all, V7X_LANES), F32), pltpu.VMEM((V7X_SUBLANES + n_all, V7X_LANES), F32),
        pltpu.VMEM((n_all, AUG_LANES), BF16), pltpu.VMEM((n_all, AUG_LANES), BF16),
        pltpu.VMEM((n_all, AUG_LANES), BF16),
        pltpu.VMEM((n_pad, V7X_LANES + AUG_LANES), BF16),
        pltpu.VMEM((t_new // tq, 2 * tq, V7X_LANES + AUG_LANES), BF16),
        pltpu.VMEM((n_pad, 2 * V7X_LANES), BF16),
        pltpu.VMEM((2 * (POOL_HIST + 1) + t_new, V7X_LANES), F32),
        pltpu.VMEM((2 * (POOL_HIST + 1) + t_new, V7X_LANES), F32),
    ]
    est = (2 * t_new * V7X_LANES * (2 + 2 + 2 + 4 + 2 + 2) + 2 * t_new * lane_pad_f32
           + 2 * past * (2 * V7X_LANES * 4 + lane_pad_f32)
           + 3 * (V7X_SUBLANES + n_all) * lane_pad_f32 + n_all * lane_pad_f32
           + 2 * n_pad * (V7X_LANES + AUG_LANES) * 2 + 2 * t_new * (V7X_LANES + AUG_LANES) * 2
           + 2 * (32 + t_new) * lane_pad_f32
           + 4 * 2 * tq * n_pad * 4 + 10 * n_all * lane_pad_f32)
    return pl.pallas_call(
        functools.partial(_attn_body, past, t_new, tq, heads),
        grid=(b, n_pairs),
        in_specs=in_specs,
        out_specs=[slab(t_new), slab(t_new)],
        out_shape=[jax.ShapeDtypeStruct((b, t_new, width), BF16),
                   jax.ShapeDtypeStruct((b, t_new, width), BF16)],
        scratch_shapes=scratch,
        compiler_params=pltpu.CompilerParams(
            dimension_semantics=("arbitrary", "arbitrary"), vmem_limit_bytes=_vmem_limit(est)),
        name="fox_attention_pool",
    )(*args)


def _sgu_body(groups, want_zv, x_ref, g_ref, w_ref, ng_ref, ws_ref, bias_ref, *rest):
    if want_zv:
        o_ref, zv_ref, zvb_scr = rest
    else:
        o_ref, zvb_scr = rest
        zv_ref = None
    width = ng_ref.shape[1]
    gdim = width // groups
    starts = range(0, x_ref.shape[0], SUB_TILE)
    zs = [jnp.dot(_rms(x_ref[r0:r0 + SUB_TILE, :], g_ref[...]).astype(BF16), w_ref[...],
                  preferred_element_type=F32) for r0 in starts]
    for r0, z in zip(starts, zs):
        rows = slice(r0, r0 + SUB_TILE)
        half_z = 0.5 * z
        z = half_z + half_z * jnp.tanh(z * (GELU_A + (GELU_A * GELU_B) * (z * z)))
        zv = _rms(z[:, width:], ng_ref[...])
        if want_zv:
            zv_ref[rows, :] = zv
        zvb_scr[rows, :] = zv.astype(BF16)
        for ch in range(SUB_TILE // SGU_CHUNK):
            rs = slice(ch * SGU_CHUNK, (ch + 1) * SGU_CHUNK)
            out_rs = slice(r0 + ch * SGU_CHUNK, r0 + (ch + 1) * SGU_CHUNK)
            for gi in range(groups):
                cs = slice(gi * gdim, (gi + 1) * gdim)
                mix = jnp.dot(ws_ref[gi], zvb_scr[out_rs, cs], preferred_element_type=F32) + bias_ref[:, cs]
                o_ref[out_rs, cs] = (z[rs, cs] * mix).astype(BF16)


def _sgu(x, g, w_in, norm_g, ws_mat, bias_full, want_zv):
    rows, d = x.shape
    width = norm_g.shape[0]
    groups = ws_mat.shape[0]
    tm = ROW_TILE
    row_spec = lambda wd: pl.BlockSpec((tm, wd), lambda i: (i, 0))
    out_specs = [row_spec(width)]
    out_shape = [jax.ShapeDtypeStruct((rows, width), BF16)]
    if want_zv:
        out_specs.append(row_spec(width))
        out_shape.append(jax.ShapeDtypeStruct((rows, width), F32))
    est = (2 * tm * d * 4 + w_in.size * 2 + ws_mat.size * 2 + bias_full.size * 4
           + 2 * tm * width * 2 + (2 * tm * width * 4 if want_zv else 0)
           + tm * width * 2 + 3 * tm * 2 * width * 4)
    res = pl.pallas_call(
        functools.partial(_sgu_body, groups, want_zv),
        grid=(rows // tm,),
        in_specs=[row_spec(d), _resident((1, d)), _resident(w_in.shape), _resident((1, width)),
                  _resident(ws_mat.shape), _resident(bias_full.shape)],
        out_specs=out_specs,
        out_shape=out_shape,
        scratch_shapes=[pltpu.VMEM((tm, width), BF16)],
        compiler_params=pltpu.CompilerParams(
            dimension_semantics=("arbitrary",), vmem_limit_bytes=_vmem_limit(est)),
        name="spatial_gating",
    )(x, g.reshape(1, d), w_in, norm_g.reshape(1, width), ws_mat, bias_full)
    return res if want_zv else (res[0], None)


def _sgu_spatial(w_s, b_s, seq_len):
    groups = w_s.shape[0]
    span = min(seq_len, SGU_CHUNK)
    idx = jnp.arange(span) // SGU_STREAM_CHUNK
    w = jnp.where((idx[None, :] <= idx[:, None])[None], w_s[:, :span, :span], 0.0)
    reps = SGU_CHUNK // span
    if reps > 1:
        w = jnp.einsum('ab,gts->gatbs', jnp.eye(reps, dtype=w.dtype), w).reshape(groups, SGU_CHUNK, SGU_CHUNK)
    bias = jnp.tile(b_s[:, :span], (1, reps))
    return w.astype(BF16), bias


def _per_head(kv, batch, seq_len, heads):
    if kv.ndim == 3:
        return jnp.transpose(kv.reshape(batch, heads, FOX_HEAD_DIM, seq_len), (0, 3, 1, 2))
    return kv.reshape(batch, seq_len, heads, FOX_HEAD_DIM)


def kernel(x_prompt, x_sample, cache_k, cache_v, cache_logf, state_pool, norm_g, ffn_w_in, ffn_w_down,
           even_w_in, even_b_f, pool_w, pool_scale, even_w_out, sgu_w_in, sgu_norm_g, sgu_w_s, sgu_b_s,
           sgu_w_out, final_g):
    b, s, d = x_prompt.shape
    bs, t, _ = x_sample.shape
    depth = norm_g.shape[0]
    heads = even_b_f.shape[1]
    width = heads * FOX_HEAD_DIM
    pool_width = pool_scale.shape[1]
    n_groups = pool_w.shape[1]
    past = cache_k.shape[2]
    sgu_width = sgu_norm_g.shape[1]
    sgu_groups = sgu_w_s.shape[1]
    sgu_gdim = sgu_width // sgu_groups

    xp = x_prompt.reshape(b * s, d)
    xs = x_sample.reshape(bs * t, d)
    kp_l, vp_l, fp_l, up_l = [], [], [], []
    ks_l, vs_l, fs_l, us_l, zs_l = [], [], [], [], []

    w_in_all = ffn_w_in.astype(BF16)
    w_down_all = ffn_w_down.astype(BF16)
    for l in range(depth):
        last = final_g if l == depth - 1 else None
        ffn_b = functools.partial(_ffn, g=norm_g[l, 2], w_in_all=w_in_all, w_down_all=w_down_all,
                                  which=(l, 1), final_g=last)
        xp = _ffn(xp, [], norm_g[l, 0], w_in_all, w_down_all, (l, 0))
        xs = _ffn(xs, [], norm_g[l, 0], w_in_all, w_down_all, (l, 0))
        if l % 2 == 0:
            e = l // 2
            w_e = even_w_in[e]
            f_cols = _replicate_gates(w_e[:, 3 * width:3 * width + heads], heads)
            w_all = jnp.concatenate([w_e[:, :3 * width], w_e[:, 3 * width + heads:], f_cols], axis=1).astype(BF16)
            bf_pad = _replicate_gates(even_b_f[e], heads).reshape(1, V7X_LANES)
            pw = pool_w[e].astype(BF16)
            psc = pool_scale[e].reshape(n_groups, 1, pool_width // n_groups)
            w_out_att = even_w_out[e, :width].astype(BF16)
            w_out_pool = even_w_out[e, width:].astype(BF16)

            q, kb, vb, k, v, u, lf, lfw = _even_proj(xp, norm_g[l, 1], w_all, bf_pad, width, heads, s)
            hist = jnp.zeros((b, POOL_HIST + 1, pool_width), F32)
            att, pool = _attention_pool(heads, q.reshape(b, s, width), kb.reshape(b, s, width),
                                        vb.reshape(b, s, width), lfw.reshape(b, s, V7X_LANES),
                                        u.reshape(b, s, pool_width), hist, pw, psc)
            xp = ffn_b(xp, [(att.reshape(b * s, width), w_out_att), (pool.reshape(b * s, pool_width), w_out_pool)])
            kp_l.append(_per_head(k, b, s, heads))
            vp_l.append(_per_head(v, b, s, heads))
            fp_l.append(jnp.transpose(lf.reshape(heads, b, s), (1, 2, 0)))
            up_l.append(u.reshape(b, s, pool_width)[:, s - POOL_HIST:])

            q, kb, vb, k, v, u, lf, lfw = _even_proj(xs, norm_g[l, 1], w_all, bf_pad, width, heads, t)
            hist = jnp.pad(state_pool[e], ((0, 0), (1, 0), (0, 0)))
            channel_major = lambda c: jnp.transpose(c, (0, 2, 3, 1)).reshape(bs, width, past)
            past_kvf = (channel_major(cache_k[e]), channel_major(cache_v[e]),
                        _replicate_gates(jnp.transpose(cache_logf[e], (0, 2, 1)), heads, axis=1))
            att, pool = _attention_pool(heads, q.reshape(bs, t, width), kb.reshape(bs, t, width),
                                        vb.reshape(bs, t, width), lfw.reshape(bs, t, V7X_LANES),
                                        u.reshape(bs, t, pool_width), hist, pw, psc, past_kvf)
            xs = ffn_b(xs, [(att.reshape(bs * t, width), w_out_att), (pool.reshape(bs * t, pool_width), w_out_pool)])
            ks_l.append(_per_head(k, bs, t, heads))
            vs_l.append(_per_head(v, bs, t, heads))
            fs_l.append(jnp.transpose(lf.reshape(heads, bs, t), (1, 2, 0)))
            u_ext = jnp.concatenate([state_pool[e], u.reshape(bs, t, pool_width)], axis=1)
            us_l.append(u_ext[:, -POOL_HIST:])
        else:
            o = l // 2
            w_in = sgu_w_in[o].astype(BF16)
            w_out = sgu_w_out[o].astype(BF16)
            ws_p, bias_p = _sgu_spatial(sgu_w_s[o], sgu_b_s[o], s)
            ws_s, bias_s = _sgu_spatial(sgu_w_s[o], sgu_b_s[o], t)
            expand = lambda bias: jnp.repeat(bias.T, sgu_gdim, axis=1)
            gated, _ = _sgu(xp, norm_g[l, 1], w_in, sgu_norm_g[o], ws_p, expand(bias_p), False)
            xp = ffn_b(xp, [(gated, w_out)])
            gated, zv = _sgu(xs, norm_g[l, 1], w_in, sgu_norm_g[o], ws_s, expand(bias_s), True)
            xs = ffn_b(xs, [(gated, w_out)])
            zs_l.append(zv.reshape(bs, t, sgu_width))

    return (xp.reshape(b, s, d), xs.reshape(bs, t, d),
            jnp.stack(kp_l), jnp.stack(vp_l), jnp.stack(fp_l), jnp.stack(up_l),
            jnp.stack(ks_l), jnp.stack(vs_l), jnp.stack(fs_l), jnp.stack(us_l), jnp.stack(zs_l))
```
